```python
import jax, jax.numpy as jnp
from jax import lax
import numpy as np

D_MODEL = 2048
BATCH = 2
SEQ = 4096
DEPTH = 2
DEC_BATCH = 128
DEC_SEQ = 4
PAST_LEN = 8192
PAGE_SIZE = 128

HEAD_DIM = 64
FOX_HEADS = 8
FOX_KV_HEADS = 2
FOX_GROUP = FOX_HEADS // FOX_KV_HEADS
SB_HEADS = 8
SB_KV_HEADS = 2
SB_GROUP = SB_HEADS // SB_KV_HEADS
MLA_HEADS = 8
Q_LORA = 512
KV_LORA = 512
NOPE_DIM = 128
ROPE_DIM = 64
V_DIM = 128
MLA_SCALE = (NOPE_DIM + ROPE_DIM) ** -0.5
MIX_WIDTH = FOX_HEADS * HEAD_DIM + SB_HEADS * HEAD_DIM + MLA_HEADS * V_DIM
D_FF = 4 * D_MODEL
BLOCK_Q = 128
ROPE_THETA = 10000.0
EPS = 1e-5
NEG_INF = -1e30
FORGET_BIAS = 2.0
ALPHA = (2 * DEPTH) ** 0.25
BETA = (8 * DEPTH) ** -0.25
SPLITS = (FOX_HEADS * HEAD_DIM, FOX_KV_HEADS * HEAD_DIM, FOX_KV_HEADS * HEAD_DIM, FOX_HEADS,
          SB_HEADS * HEAD_DIM, SB_KV_HEADS * HEAD_DIM, SB_KV_HEADS * HEAD_DIM,
          Q_LORA, KV_LORA, ROPE_DIM)
IN_COLS = sum(SPLITS)

kernel_name = "hybrid_fox_stickbreak_mla_step"


def layer_norm(x, g, b):
    xf = x.astype(jnp.float32)
    mu = jnp.mean(xf, axis=-1, keepdims=True)
    var = jnp.mean(jnp.square(xf - mu), axis=-1, keepdims=True)
    return ((xf - mu) * lax.rsqrt(var + EPS) * g.astype(jnp.float32) + b.astype(jnp.float32)).astype(x.dtype)


def unit_rms(x):
    xf = x.astype(jnp.float32)
    return (xf * lax.rsqrt(jnp.mean(jnp.square(xf), axis=-1, keepdims=True) + EPS)).astype(x.dtype)


def rms_norm(x, g):
    return unit_rms(x) * g.astype(x.dtype)


def rope(x, pos):
    half = x.shape[-1] // 2
    inv_freq = ROPE_THETA ** (-jnp.arange(half, dtype=jnp.float32) / half)
    ang = pos.astype(jnp.float32)[:, None] * inv_freq[None, :]
    shape = (1, ang.shape[0]) + (1,) * (x.ndim - 3) + (half,)
    cos = jnp.cos(ang).reshape(shape)
    sin = jnp.sin(ang).reshape(shape)
    xf = x.astype(jnp.float32)
    x1, x2 = xf[..., :half], xf[..., half:]
    return jnp.concatenate([x1 * cos - x2 * sin, x1 * sin + x2 * cos], axis=-1).astype(x.dtype)


def causal_mask(tq, tk, strict):
    qpos = (tk - tq) + jnp.arange(tq)
    kpos = jnp.arange(tk)
    return (kpos[None, :] < qpos[:, None]) if strict else (kpos[None, :] <= qpos[:, None])


def fox_attention(q, k, v, logf):
    b, tq, tk = q.shape[0], q.shape[1], k.shape[1]
    logf = logf.astype(jnp.float32)
    r = lax.cumsum(logf, axis=1, reverse=True) - logf
    r = jnp.transpose(r.reshape(b, tk, FOX_KV_HEADS, FOX_GROUP), (0, 2, 3, 1))
    bias = r[:, :, :, None, :] - r[:, :, :, tk - tq:, None]
    s = jnp.einsum('bqkgd,bskd->bkgqs', q, k).astype(jnp.float32) * (HEAD_DIM ** -0.5) + bias
    s = jnp.where(causal_mask(tq, tk, False), s, NEG_INF)
    p = jax.nn.softmax(s, axis=-1).astype(v.dtype)
    return jnp.einsum('bkgqs,bskd->bqkgd', p, v)


def stick_breaking_attention(q, k, v):
    tq, tk = q.shape[1], k.shape[1]
    z = jnp.einsum('bqkgd,bskd->bkgqs', q, k).astype(jnp.float32) * (HEAD_DIM ** -0.5)
    mask = causal_mask(tq, tk, True)
    log_rest = jnp.where(mask, jax.nn.log_sigmoid(-z), 0.0)
    key_axis = log_rest.ndim - 1
    later = lax.cumsum(log_rest, axis=key_axis, reverse=True) - log_rest
    w = jnp.where(mask, jnp.exp(jax.nn.log_sigmoid(z) + later), 0.0).astype(v.dtype)
    return jnp.einsum('bkgqs,bskd->bqkgd', w, v)


def latent_attention(q_lat, q_rope, ckv, krope):
    tq, tk = q_lat.shape[1], ckv.shape[1]
    s = (jnp.einsum('bqhr,bsr->bhqs', q_lat, ckv)
         + jnp.einsum('bqhd,bsd->bhqs', q_rope, krope)).astype(jnp.float32) * MLA_SCALE
    s = jnp.where(causal_mask(tq, tk, False), s, NEG_INF)
    p = jax.nn.softmax(s, axis=-1).astype(ckv.dtype)
    return jnp.einsum('bhqs,bsr->bqhr', p, ckv)


def project(x, pos, lp):
    b, t, _ = x.shape
    u = jnp.einsum('btd,dc->btc', x, lp['w_in'])
    points = [int(p) for p in np.cumsum(SPLITS)[:-1]]
    fq, fk, fv, ff, sq, sk, sv, cq, ckv, kr = jnp.split(u, points, axis=-1)
    fox_q = fq.reshape(b, t, FOX_KV_HEADS, FOX_GROUP, HEAD_DIM)
    fox_k = fk.reshape(b, t, FOX_KV_HEADS, HEAD_DIM)
    fox_v = fv.reshape(b, t, FOX_KV_HEADS, HEAD_DIM)
    fox_logf = jax.nn.log_sigmoid(ff.astype(jnp.float32) + lp['b_f'].astype(jnp.float32))
    sb_q = sq.reshape(b, t, SB_KV_HEADS, SB_GROUP, HEAD_DIM)
    sb_k = sk.reshape(b, t, SB_KV_HEADS, HEAD_DIM)
    sb_v = sv.reshape(b, t, SB_KV_HEADS, HEAD_DIM)
    q = jnp.einsum('btc,ce->bte', rms_norm(cq, lp['g_cq']), lp['w_uq'])
    q = q.reshape(b, t, MLA_HEADS, NOPE_DIM + ROPE_DIM)
    q_rope = rope(q[..., NOPE_DIM:], pos)
    q_lat = jnp.einsum('bthn,rhn->bthr', q[..., :NOPE_DIM], lp['w_uk'])
    mla_ckv = rms_norm(ckv, lp['g_ckv'])
    mla_krope = rope(kr, pos)
    queries = (fox_q, sb_q, q_lat, q_rope)
    rows = (fox_k, fox_v, fox_logf, sb_k, sb_v, mla_ckv, mla_krope)
    return queries, rows


def attend_groups(queries, keys):
    fox_q, sb_q, q_lat, q_rope = queries
    fox_k, fox_v, fox_logf, sb_k, sb_v, ckv, krope = keys
    return (fox_attention(fox_q, fox_k, fox_v, fox_logf),
            stick_breaking_attention(sb_q, sb_k, sb_v),
            latent_attention(q_lat, q_rope, ckv, krope))


def token_mixers(x, pos, lp, past):
    b, t, _ = x.shape
    queries, rows = project(x, pos, lp)
    if past is None:
        outs = []
        for i in range(t // BLOCK_Q):
            lo, hi = i * BLOCK_Q, (i + 1) * BLOCK_Q
            outs.append(attend_groups(tuple(qq[:, lo:hi] for qq in queries),
                                      tuple(rr[:, :hi] for rr in rows)))
        fox_o = jnp.concatenate([o[0] for o in outs], axis=1)
        sb_o = jnp.concatenate([o[1] for o in outs], axis=1)
        mla_lat = jnp.concatenate([o[2] for o in outs], axis=1)
    else:
        keys = tuple(jnp.concatenate([pp.astype(rr.dtype), rr], axis=1) for pp, rr in zip(past, rows))
        fox_o, sb_o, mla_lat = attend_groups(queries, keys)
    mla_o = jnp.einsum('bthr,rhv->bthv', mla_lat, lp['w_uv'])
    groups = (fox_o.reshape(b, t, -1), sb_o.reshape(b, t, -1), mla_o.reshape(b, t, -1))
    o = jnp.concatenate([unit_rms(g) for g in groups], axis=-1) * lp['g_o'].astype(x.dtype)
    return jnp.einsum('btc,cd->btd', o, lp['w_o']), rows


def decoder_layer(x, pos, lp, past):
    mix, rows = token_mixers(x, pos, lp, past)
    x = layer_norm(ALPHA * x + mix, lp['ln1_g'], lp['ln1_b'])
    h = jnp.square(jax.nn.relu(jnp.einsum('btd,df->btf', x, lp['w_ff1'])))
    x = layer_norm(ALPHA * x + jnp.einsum('btf,fd->btd', h, lp['w_ff2']), lp['ln2_g'], lp['ln2_b'])
    return x, rows


def gather_pages(pool, layer, page_table):
    g = pool[layer, page_table]
    return g.reshape((g.shape[0], g.shape[1] * g.shape[2]) + g.shape[3:])


def setup_inputs(seed: int = 0) -> dict:
    key = jax.random.key(seed)
    ks = jax.random.split(key, 32)
    f32 = jnp.float32
    n_pages = PAST_LEN // PAGE_SIZE
    n_used = DEC_BATCH * n_pages
    n_pool = n_used + n_used // 4
    pool = (DEPTH, n_pool, PAGE_SIZE)

    def nrm(k, shape, scale=1.0):
        return jax.random.normal(k, shape, f32) * scale

    return {
        "x_prompt": nrm(ks[0], (BATCH, SEQ, D_MODEL)),
        "x_sample": nrm(ks[1], (DEC_BATCH, DEC_SEQ, D_MODEL)),
        "cache_fox_k": nrm(ks[2], pool + (FOX_KV_HEADS, HEAD_DIM)),
        "cache_fox_v": nrm(ks[3], pool + (FOX_KV_HEADS, HEAD_DIM)),
        "cache_fox_logf": jax.nn.log_sigmoid(FORGET_BIAS + nrm(ks[4], pool + (FOX_HEADS,))),
        "cache_sb_k": nrm(ks[5], pool + (SB_KV_HEADS, HEAD_DIM)),
        "cache_sb_v": nrm(ks[6], pool + (SB_KV_HEADS, HEAD_DIM)),
        "cache_mla_ckv": nrm(ks[7], pool + (KV_LORA,)),
        "cache_mla_krope": nrm(ks[8], pool + (ROPE_DIM,)),
        "page_table": jax.random.permutation(ks[9], n_pool)[:n_used].reshape(DEC_BATCH, n_pages).astype(jnp.int32),
        "w_in": nrm(ks[10], (DEPTH, D_MODEL, IN_COLS), D_MODEL ** -0.5),
        "b_f": FORGET_BIAS + nrm(ks[11], (DEPTH, FOX_HEADS), 0.1),
        "g_cq": 1.0 + nrm(ks[12], (DEPTH, Q_LORA), 0.02),
        "w_uq": nrm(ks[13], (DEPTH, Q_LORA, MLA_HEADS * (NOPE_DIM + ROPE_DIM)), Q_LORA ** -0.5),
        "g_ckv": 1.0 + nrm(ks[14], (DEPTH, KV_LORA), 0.02),
        "w_uk": nrm(ks[15], (DEPTH, KV_LORA, MLA_HEADS, NOPE_DIM), KV_LORA ** -0.5),
        "w_uv": nrm(ks[16], (DEPTH, KV_LORA, MLA_HEADS, V_DIM), KV_LORA ** -0.5),
        "g_o": 1.0 + nrm(ks[17], (DEPTH, MIX_WIDTH), 0.02),
        "w_o": nrm(ks[18], (DEPTH, MIX_WIDTH, D_MODEL), BETA * MIX_WIDTH ** -0.5),
        "ln1_g": 1.0 + nrm(ks[19], (DEPTH, D_MODEL), 0.02),
        "ln1_b": nrm(ks[20], (DEPTH, D_MODEL), 0.02),
        "w_ff1": nrm(ks[21], (DEPTH, D_MODEL, D_FF), D_MODEL ** -0.5),
        "w_ff2": nrm(ks[22], (DEPTH, D_FF, D_MODEL), BETA * D_FF ** -0.5),
        "ln2_g": 1.0 + nrm(ks[23], (DEPTH, D_MODEL), 0.02),
        "ln2_b": nrm(ks[24], (DEPTH, D_MODEL), 0.02),
    }


def reference(x_prompt, x_sample, cache_fox_k, cache_fox_v, cache_fox_logf, cache_sb_k, cache_sb_v,
              cache_mla_ckv, cache_mla_krope, page_table, w_in, b_f, g_cq, w_uq, g_ckv, w_uk, w_uv,
              g_o, w_o, ln1_g, ln1_b, w_ff1, w_ff2, ln2_g, ln2_b):
    pos_prompt = jnp.arange(x_prompt.shape[1])
    pos_sample = PAST_LEN + jnp.arange(x_sample.shape[1])
    pools = (cache_fox_k, cache_fox_v, cache_fox_logf, cache_sb_k, cache_sb_v, cache_mla_ckv, cache_mla_krope)
    xp, xs = x_prompt, x_sample
    rows_p, rows_s = [], []
    for l in range(DEPTH):
        lp = {'w_in': w_in[l], 'b_f': b_f[l], 'g_cq': g_cq[l], 'w_uq': w_uq[l], 'g_ckv': g_ckv[l],
              'w_uk': w_uk[l], 'w_uv': w_uv[l], 'g_o': g_o[l], 'w_o': w_o[l], 'ln1_g': ln1_g[l],
              'ln1_b': ln1_b[l], 'w_ff1': w_ff1[l], 'w_ff2': w_ff2[l], 'ln2_g': ln2_g[l], 'ln2_b': ln2_b[l]}
        xp, rp = decoder_layer(xp, pos_prompt, lp, None)
        past = tuple(gather_pages(pl, l, page_table) for pl in pools)
        xs, rs = decoder_layer(xs, pos_sample, lp, past)
        rows_p.append(rp)
        rows_s.append(rs)
    p_fox_k, p_fox_v, p_fox_logf, p_sb_k, p_sb_v, p_mla_ckv, p_mla_krope = (
        jnp.stack([r[j] for r in rows_p]) for j in range(7))
    s_fox_k, s_fox_v, s_fox_logf, s_sb_k, s_sb_v, s_mla_ckv, s_mla_krope = (
        jnp.stack([r[j] for r in rows_s]) for j in range(7))
    return (xp, xs, p_fox_k, p_fox_v, p_fox_logf, p_sb_k, p_sb_v, p_mla_ckv, p_mla_krope,
            s_fox_k, s_fox_v, s_fox_logf, s_sb_k, s_sb_v, s_mla_ckv, s_mla_krope)
```

```python
import functools

import numpy as np
import jax
import jax.numpy as jnp
from jax import lax
from jax.experimental import pallas as pl
from jax.experimental.pallas import tpu as pltpu

F32 = jnp.float32
BF16 = jnp.bfloat16

HEAD_DIM = 64
FOX_HEADS = 8
FOX_KV_HEADS = 2
SB_HEADS = 8
SB_KV_HEADS = 2
MLA_HEADS = 8
Q_LORA = 512
KV_LORA = 512
NOPE_DIM = 128
ROPE_DIM = 64
V_DIM = 128
PAGE_SIZE = 128
DEPTH = 2
MLA_SCALE = (NOPE_DIM + ROPE_DIM) ** -0.5
ATTN_SCALE = HEAD_DIM ** -0.5
ROPE_THETA = 10000.0
EPS = 1e-5
NEG_INF = -1e30
ALPHA = (2 * DEPTH) ** 0.25

LANES = 128
V7X_VMEM_BYTES = 64 * 1024 * 1024

_SEG = {}
_off = 0
for _name, _w in (("fq", 512), ("sq", 512), ("cq", 512), ("ckv", 512), ("fk", 128), ("fv", 128),
                  ("sk", 128), ("sv", 128), ("kr", 128), ("ff", 128)):
    _SEG[_name] = (_off, _w)
    _off += _w
PACKED_COLS = _off

ROW_TILE = 512
PAGES_PER_STEP = 8


def _cparams(sem, vmem_bytes):
    return pltpu.CompilerParams(dimension_semantics=sem, vmem_limit_bytes=int(vmem_bytes))


def _log_sigmoid(x):
    return jnp.minimum(x, 0.0) - jnp.log1p(jnp.exp(-jnp.abs(x)))


def _split3(a):
    a1 = a.astype(BF16)
    r1 = a - a1.astype(F32)
    a2 = r1.astype(BF16)
    a3 = (r1 - a2.astype(F32)).astype(BF16)
    return a1, a2, a3


def _dot_exact01(a, m01):
    out = None
    for piece in _split3(a):
        d = jnp.dot(piece, m01, preferred_element_type=F32)
        out = d if out is None else out + d
    return out


def _dot_nt(a, b):
    return lax.dot_general(a, b, (((1,), (1,)), ((), ())), preferred_element_type=F32)


def _lane_iota(shape):
    return lax.broadcasted_iota(jnp.int32, shape, len(shape) - 1)


def _rope_swap(x):
    lane = _lane_iota(x.shape)
    return jnp.where((lane % 64) < 32, pltpu.roll(x, 96, 1), pltpu.roll(x, 32, 1))


def _proj_kernel(x_ref, w_ref, o_ref):
    o_ref[...] = jnp.dot(x_ref[...].astype(BF16), w_ref[...], preferred_element_type=F32)


def _proj(x, w):
    m, k = x.shape
    n = w.shape[1]
    tm = ROW_TILE
    vmem = 2 * (tm * k * 4 + k * n * 2 + tm * n * 4) + tm * k * 2 + tm * n * 4
    return pl.pallas_call(
        _proj_kernel,
        grid=(m // tm,),
        in_specs=[pl.BlockSpec((tm, k), lambda i: (i, 0)),
                  pl.BlockSpec((k, n), lambda i: (0, 0))],
        out_specs=pl.BlockSpec((tm, n), lambda i: (i, 0)),
        out_shape=jax.ShapeDtypeStruct((m, n), F32),
        compiler_params=_cparams(("parallel",), vmem + (4 << 20)),
        name="proj",
    )(x, w)


def _prep_kernel(cq_ref, ckv_ref, kr_ref, ff_ref, bf_ref, gcq_ref, gckv_ref, wuq_ref, wuk_ref,
                 wuv_ref, cos_ref, sin_ref, tri_ref,
                 logf_ref, cs_ref, ckvn_ref, krope_ref, qm_ref, km_ref, vm_ref, car_ref,
                 *, tiles_per_seq):
    i = pl.program_id(0)
    tm = cq_ref.shape[0]
    lane = _lane_iota((tm, LANES))

    lf = _log_sigmoid(ff_ref[...] + bf_ref[...])
    logf_ref[...] = lf[:, :FOX_HEADS]

    @pl.when(i % tiles_per_seq == 0)
    def _():
        car_ref[...] = jnp.zeros_like(car_ref)

    lft = jnp.where(lane < FOX_HEADS, lf, 0.0).T[:FOX_HEADS, :]
    cum = _dot_exact01(lft, tri_ref[...]) + car_ref[:, :1]
    cs_ref[...] = cum
    car_ref[...] = jnp.broadcast_to(cum[:, tm - 1:tm], car_ref.shape)

    cos = cos_ref[...]
    sin = sin_ref[...]

    cq = cq_ref[...]
    cqn = cq * lax.rsqrt(jnp.mean(cq * cq, axis=-1, keepdims=True) + EPS) * gcq_ref[...]
    q = jnp.dot(cqn.astype(BF16), wuq_ref[...], preferred_element_type=F32)
    nope_w = MLA_HEADS * NOPE_DIM
    ropes = []
    for g in range(MLA_HEADS * ROPE_DIM // LANES):
        xg = q[:, nope_w + g * LANES: nope_w + (g + 1) * LANES]
        ropes.append(xg * cos + _rope_swap(xg) * sin)
    for h in range(MLA_HEADS):
        qm_ref[:, 256 * h: 256 * h + 128] = (q[:, 128 * h: 128 * h + 128] * MLA_SCALE).astype(BF16)
        grp = ropes[h // 2]
        if h % 2 == 1:
            grp = pltpu.roll(grp, 64, 1)
        qm_ref[:, 256 * h + 128: 256 * h + 256] = jnp.where(lane < ROPE_DIM, grp * MLA_SCALE, 0.0).astype(BF16)

    ckv = ckv_ref[...]
    ckvn = ckv * lax.rsqrt(jnp.mean(ckv * ckv, axis=-1, keepdims=True) + EPS) * gckv_ref[...]
    ckvn_ref[...] = ckvn
    ckvb = ckvn.astype(BF16)
    knope = jnp.dot(ckvb, wuk_ref[...], preferred_element_type=F32)
    vm_ref[...] = jnp.dot(ckvb, wuv_ref[...], preferred_element_type=F32).astype(BF16)

    kr = kr_ref[...]
    krr = kr * cos + _rope_swap(kr) * sin
    krope_ref[...] = krr[:, :ROPE_DIM]
    kpiece = jnp.where(lane < ROPE_DIM, krr, 0.0).astype(BF16)
    for h in range(MLA_HEADS):
        km_ref[:, 256 * h: 256 * h + 128] = knope[:, 128 * h: 128 * h + 128].astype(BF16)
        km_ref[:, 256 * h + 128: 256 * h + 256] = kpiece


def _prep(u, bf_row, gcq, gckv, wuq, wuk, wuv, cos_t, sin_t, tri, tiles_per_seq):
    m = u.shape[0]
    tm = ROW_TILE
    nt = m // tm

    def seg(name):
        off, w = _SEG[name]
        return pl.BlockSpec((tm, w), lambda i, _c=off // w: (i, _c))

    def const(shape):
        return pl.BlockSpec(shape, lambda i: tuple(0 for _ in shape))

    row = lambda w: pl.BlockSpec((tm, w), lambda i: (i, 0))
    out_shape = (
        jax.ShapeDtypeStruct((m, FOX_HEADS), F32),
        jax.ShapeDtypeStruct((nt, FOX_HEADS, tm), F32),
        jax.ShapeDtypeStruct((m, KV_LORA), F32),
        jax.ShapeDtypeStruct((m, ROPE_DIM), F32),
        jax.ShapeDtypeStruct((m, MLA_HEADS * 256), BF16),
        jax.ShapeDtypeStruct((m, MLA_HEADS * 256), BF16),
        jax.ShapeDtypeStruct((m, MLA_HEADS * V_DIM), BF16),
    )
    out_specs = (
        row(FOX_HEADS),
        pl.BlockSpec((None, FOX_HEADS, tm), lambda i: (i, 0, 0)),
        row(KV_LORA), row(ROPE_DIM), row(MLA_HEADS * 256), row(MLA_HEADS * 256), row(MLA_HEADS * V_DIM),
    )
    return pl.pallas_call(
        functools.partial(_prep_kernel, tiles_per_seq=tiles_per_seq),
        grid=(nt,),
        in_specs=[seg("cq"), seg("ckv"), seg("kr"), seg("ff"),
                  const((1, LANES)), const((1, Q_LORA)), const((1, KV_LORA)),
                  const(wuq.shape), const(wuk.shape), const(wuv.shape),
                  row(LANES), row(LANES), const(tri.shape)],
        out_specs=out_specs,
        out_shape=out_shape,
        scratch_shapes=[pltpu.VMEM((FOX_HEADS, LANES), F32)],
        compiler_params=_cparams(("arbitrary",), 48 << 20),
        name="prep",
    )(u, u, u, u, bf_row, gcq, gckv, wuq, wuk, wuv, cos_t, sin_t, tri)


def _qlat_kernel(q_ref, w_ref, o_ref):
    o_ref[...] = jnp.dot(q_ref[:, :NOPE_DIM], w_ref[...], preferred_element_type=F32).astype(o_ref.dtype)


def _qlat(qm, wukt, row_block, rows):
    return pl.pallas_call(
        _qlat_kernel,
        grid=(MLA_HEADS,),
        in_specs=[pl.BlockSpec((rows, 256), lambda h: (row_block, h)),
                  pl.BlockSpec((None, NOPE_DIM, KV_LORA), lambda h: (h, 0, 0))],
        out_specs=pl.BlockSpec((None, rows, KV_LORA), lambda h: (h, 0, 0)),
        out_shape=jax.ShapeDtypeStruct((MLA_HEADS, rows, KV_LORA), BF16),
        compiler_params=_cparams(("parallel",), 16 << 20),
        name="qlat",
    )(qm, wukt)


def _latv_kernel(lat_ref, w_ref, o_ref):
    o_ref[...] = jnp.dot(lat_ref[...].astype(BF16), w_ref[...], preferred_element_type=F32)


def _latv(lat, wuv_h):
    rows = lat.shape[1]
    return pl.pallas_call(
        _latv_kernel,
        grid=(MLA_HEADS,),
        in_specs=[pl.BlockSpec((None, rows, KV_LORA), lambda h: (h, 0, 0)),
                  pl.BlockSpec((None, KV_LORA, V_DIM), lambda h: (h, 0, 0))],
        out_specs=pl.BlockSpec((rows, V_DIM), lambda h: (0, h)),
        out_shape=jax.ShapeDtypeStruct((rows, MLA_HEADS * V_DIM), F32),
        compiler_params=_cparams(("parallel",), 16 << 20),
        name="latv",
    )(lat, wuv_h)


def _layer_norm(y, g, b):
    mu = jnp.mean(y, axis=-1, keepdims=True)
    d = y - mu
    var = jnp.mean(d * d, axis=-1, keepdims=True)
    return d * lax.rsqrt(var + EPS) * g + b


def _unit_rms(x):
    return x * lax.rsqrt(jnp.mean(x * x, axis=-1, keepdims=True) + EPS)


def _outproj_kernel(fox_ref, sb_ref, mla_ref, x_ref, go_ref, wo_ref, g_ref, b_ref, o_ref, ob_ref):
    go = go_ref[...]
    w0 = FOX_HEADS * HEAD_DIM
    w1 = w0 + SB_HEADS * HEAD_DIM
    parts = ((fox_ref, 0, w0), (sb_ref, w0, w1), (mla_ref, w1, go.shape[1]))
    mix = None
    for ref, lo, hi in parts:
        o = (_unit_rms(ref[...]) * go[:, lo:hi]).astype(BF16)
        d = jnp.dot(o, wo_ref[lo:hi, :], preferred_element_type=F32)
        mix = d if mix is None else mix + d
    y = _layer_norm(ALPHA * x_ref[...] + mix, g_ref[...], b_ref[...])
    o_ref[...] = y
    ob_ref[...] = y.astype(BF16)


def _outproj(fox_o, sb_o, mla_o, x, go, wo, g, b):
    m, d = x.shape
    tm = ROW_TILE
    row = lambda w: pl.BlockSpec((tm, w), lambda i: (i, 0))
    const = lambda shape: pl.BlockSpec(shape, lambda i: (0, 0))
    return pl.pallas_call(
        _outproj_kernel,
        grid=(m // tm,),
        in_specs=[row(fox_o.shape[1]), row(sb_o.shape[1]), row(mla_o.shape[1]), row(d),
                  const((1, d)), const(wo.shape), const((1, d)), const((1, d))],
        out_specs=(row(d), row(d)),
        out_shape=(jax.ShapeDtypeStruct((m, d), F32), jax.ShapeDtypeStruct((m, d), BF16)),
        compiler_params=_cparams(("parallel",), 56 << 20),
        name="outproj",
    )(fox_o, sb_o, mla_o, x, go, wo, g, b)


def _ff1_kernel(x_ref, w_ref, o_ref):
    h = jnp.maximum(jnp.dot(x_ref[...], w_ref[...], preferred_element_type=F32), 0.0)
    o_ref[...] = (h * h).astype(o_ref.dtype)


def _ff1(xb, w1, tn=2048):
    m, k = xb.shape
    n = w1.shape[1]
    tm = ROW_TILE
    return pl.pallas_call(
        _ff1_kernel,
        grid=(n // tn, m // tm),
        in_specs=[pl.BlockSpec((tm, k), lambda j, i: (i, 0)),
                  pl.BlockSpec((k, tn), lambda j, i: (0, j))],
        out_specs=pl.BlockSpec((tm, tn), lambda j, i: (i, j)),
        out_shape=jax.ShapeDtypeStruct((m, n), BF16),
        compiler_params=_cparams(("parallel", "parallel"), 48 << 20),
        name="ff1",
    )(xb, w1)


def _ff2_kernel(h_ref, w_ref, x_ref, g_ref, b_ref, o_ref, acc_ref):
    k = pl.program_id(1)

    @pl.when(k == 0)
    def _():
        acc_ref[...] = jnp.zeros_like(acc_ref)

    acc_ref[...] += jnp.dot(h_ref[...], w_ref[...], preferred_element_type=F32)

    @pl.when(k == pl.num_programs(1) - 1)
    def _():
        o_ref[...] = _layer_norm(ALPHA * x_ref[...] + acc_ref[...], g_ref[...], b_ref[...])


def _ff2(h, w2, x, g, b, tk=2048):
    m, kk = h.shape
    d = w2.shape[1]
    tm = ROW_TILE
    return pl.pallas_call(
        _ff2_kernel,
        grid=(m // tm, kk // tk),
        in_specs=[pl.BlockSpec((tm, tk), lambda i, k: (i, k)),
                  pl.BlockSpec((tk, d), lambda i, k: (k, 0)),
                  pl.BlockSpec((tm, d), lambda i, k: (i, 0)),
                  pl.BlockSpec((1, d), lambda i, k: (0, 0)),
                  pl.BlockSpec((1, d), lambda i, k: (0, 0))],
        out_specs=pl.BlockSpec((tm, d), lambda i, k: (i, 0)),
        out_shape=jax.ShapeDtypeStruct((m, d), F32),
        scratch_shapes=[pltpu.VMEM((tm, d), F32)],
        compiler_params=_cparams(("parallel", "arbitrary"), 48 << 20),
        name="ff2",
    )(h, w2, x, g, b)


def _build_qbd(q_ref, qbd_ref, tq):
    lane = _lane_iota((tq, LANES))
    group = FOX_HEADS // FOX_KV_HEADS
    for h in range(FOX_HEADS):
        g = h // group
        src = q_ref[:, LANES * (h // 2): LANES * (h // 2 + 1)] * ATTN_SCALE
        if (h % 2) != g:
            src = pltpu.roll(src, 64, 1)
        in_half = (lane >= 64 * g) & (lane < 64 * (g + 1))
        qbd_ref[h * tq:(h + 1) * tq, :] = jnp.where(in_half, src, 0.0).astype(BF16)


def _store_heads(o_ref, acc, tq):
    lane = _lane_iota((tq, LANES))
    group = FOX_HEADS // FOX_KV_HEADS
    for pair in range(FOX_HEADS // 2):
        pieces = []
        for h in (2 * pair, 2 * pair + 1):
            x = acc[h * tq:(h + 1) * tq, :]
            if (h % 2) != (h // group):
                x = pltpu.roll(x, 64, 1)
            pieces.append(x)
        o_ref[:, LANES * pair: LANES * (pair + 1)] = jnp.where(lane < 64, pieces[0], pieces[1])


def _fox_prefill_kernel(q_ref, k_ref, v_ref, cs_ref, o_ref,
                        qbd_ref, kb_ref, vb_ref, p_ref, m_ref, l_ref, acc_ref, *, tq, tk):
    qi = pl.program_id(1)
    rows = FOX_HEADS * tq

    @pl.when(qi == 0)
    def _():
        kb_ref[...] = k_ref[...].astype(BF16)
        vb_ref[...] = v_ref[...].astype(BF16)

    _build_qbd(q_ref, qbd_ref, tq)
    m_ref[...] = jnp.full_like(m_ref, NEG_INF)
    l_ref[...] = jnp.zeros_like(l_ref)
    acc_ref[...] = jnp.zeros_like(acc_ref)

    blocks_per_tk = tk // tq
    n_full = qi // blocks_per_tk

    def step(kb, masked):
        k0 = pl.multiple_of(kb * tk, tk)
        s = _dot_nt(qbd_ref[...], kb_ref[pl.ds(k0, tk), :])
        negc = -cs_ref[kb]
        if masked:
            rpos = qi * tq + lax.broadcasted_iota(jnp.int32, (tq, tk), 0)
            cpos = kb * tk + lax.broadcasted_iota(jnp.int32, (tq, tk), 1)
            keep = cpos <= rpos
        for h in range(FOX_HEADS):
            sl = slice(h * tq, (h + 1) * tq)
            sh = s[sl, :] + negc[h:h + 1, :]
            if masked:
                sh = jnp.where(keep, sh, NEG_INF)
            m_prev = m_ref[sl, :]
            m_new = jnp.maximum(m_prev, jnp.max(sh, axis=-1, keepdims=True))
            alpha = jnp.exp(m_prev - m_new)
            p = jnp.exp(sh - m_new[:, :1])
            l_ref[sl, :] = alpha * l_ref[sl, :] + jnp.sum(p, axis=-1, keepdims=True)
            m_ref[sl, :] = m_new
            acc_ref[sl, :] = acc_ref[sl, :] * alpha
            p_ref[sl, :] = p.astype(BF16)
        acc_ref[...] += jnp.dot(p_ref[...], vb_ref[pl.ds(k0, tk), :], preferred_element_type=F32)

    def body(kb, carry):
        step(kb, False)
        return carry

    lax.fori_loop(0, n_full, body, 0)
    step(n_full, True)

    _store_heads(o_ref, acc_ref[...] / l_ref[...], tq)


def _fox_prefill(u, cs, batch, seq, tq=128):
    tk = ROW_TILE
    nq = seq // tq
    rows = FOX_HEADS * tq
    fq_c = _SEG["fq"][0] // 512
    fk_c = _SEG["fk"][0] // 128
    fv_c = _SEG["fv"][0] // 128
    return pl.pallas_call(
        functools.partial(_fox_prefill_kernel, tq=tq, tk=tk),
        grid=(batch, nq),
        in_specs=[pl.BlockSpec((tq, 512), lambda b, i: (b * nq + i, fq_c)),
                  pl.BlockSpec((seq, 128), lambda b, i: (b, fk_c)),
                  pl.BlockSpec((seq, 128), lambda b, i: (b, fv_c)),
                  pl.BlockSpec((seq // tk, FOX_HEADS, tk), lambda b, i: (b, 0, 0))],
        out_specs=pl.BlockSpec((tq, 512), lambda b, i: (b * nq + i, 0)),
        out_shape=jax.ShapeDtypeStruct((batch * seq, 512), F32),
        scratch_shapes=[pltpu.VMEM((rows, LANES), BF16),
                        pltpu.VMEM((seq, LANES), BF16),
                        pltpu.VMEM((seq, LANES), BF16),
                        pltpu.VMEM((rows, tk), BF16),
                        pltpu.VMEM((rows, LANES), F32),
                        pltpu.VMEM((rows, LANES), F32),
                        pltpu.VMEM((rows, LANES), F32)],
        compiler_params=_cparams(("arbitrary", "arbitrary"), 40 << 20),
        name="fox_prefill",
    )(u, u, u, cs)


def _sb_prefill_kernel(q_ref, k_ref, v_ref, urev_ref, o_ref,
                       qbd_ref, kb_ref, vb_ref, r_ref, acc_ref, *, tq, tk):
    qi = pl.program_id(1)

    @pl.when(qi == 0)
    def _():
        kb_ref[...] = k_ref[...].astype(BF16)
        vb_ref[...] = v_ref[...].astype(BF16)

    _build_qbd(q_ref, qbd_ref, tq)
    r_ref[...] = jnp.zeros_like(r_ref)
    acc_ref[...] = jnp.zeros_like(acc_ref)

    blocks_per_tk = tk // tq
    kb_diag = qi // blocks_per_tk

    def step(kb, masked):
        k0 = pl.multiple_of(kb * tk, tk)
        z = _dot_nt(qbd_ref[...], kb_ref[pl.ds(k0, tk), :])
        lr = -(jnp.maximum(z, 0.0) + jnp.log1p(jnp.exp(-jnp.abs(z))))
        if masked:
            rows = SB_HEADS * tq
            rpos = qi * tq + (lax.broadcasted_iota(jnp.int32, (rows, tk), 0) & (tq - 1))
            cpos = kb * tk + lax.broadcasted_iota(jnp.int32, (rows, tk), 1)
            keep = cpos < rpos
            lr = jnp.where(keep, lr, 0.0)
        lr1 = lr.astype(BF16)
        lr2 = (lr - lr1.astype(F32)).astype(BF16)
        urev = urev_ref[...]
        cum = (jnp.dot(lr1, urev, preferred_element_type=F32)
               + jnp.dot(lr2, urev, preferred_element_type=F32))
        w = jnp.exp(z + cum + r_ref[:, :1])
        if masked:
            w = jnp.where(keep, w, 0.0)
        r_ref[...] += cum[:, :1]
        acc_ref[...] += jnp.dot(w.astype(BF16), vb_ref[pl.ds(k0, tk), :], preferred_element_type=F32)

    step(kb_diag, True)

    def body(j, carry):
        step(kb_diag - 1 - j, False)
        return carry

    lax.fori_loop(0, kb_diag, body, 0)
    _store_heads(o_ref, acc_ref[...], tq)


def _sb_prefill(u, urev, batch, seq, tq=128):
    tk = urev.shape[0]
    nq = seq // tq
    rows = SB_HEADS * tq
    sq_c = _SEG["sq"][0] // 512
    sk_c = _SEG["sk"][0] // 128
    sv_c = _SEG["sv"][0] // 128
    return pl.pallas_call(
        functools.partial(_sb_prefill_kernel, tq=tq, tk=tk),
        grid=(batch, nq),
        in_specs=[pl.BlockSpec((tq, 512), lambda b, i: (b * nq + i, sq_c)),
                  pl.BlockSpec((seq, 128), lambda b, i: (b, sk_c)),
                  pl.BlockSpec((seq, 128), lambda b, i: (b, sv_c)),
                  pl.BlockSpec((tk, tk), lambda b, i: (0, 0))],
        out_specs=pl.BlockSpec((tq, 512), lambda b, i: (b * nq + i, 0)),
        out_shape=jax.ShapeDtypeStruct((batch * seq, 512), F32),
        scratch_shapes=[pltpu.VMEM((rows, LANES), BF16),
                        pltpu.VMEM((seq, LANES), BF16),
                        pltpu.VMEM((seq, LANES), BF16),
                        pltpu.VMEM((rows, LANES), F32),
                        pltpu.VMEM((rows, LANES), F32)],
        compiler_params=_cparams(("arbitrary", "arbitrary"), 40 << 20),
        name="sb_prefill",
    )(u, u, u, urev)


def _mla_prefill_kernel(q_ref, k_ref, v_ref, o_ref, m_ref, l_ref, acc_ref, *, tq, tk):
    qi = pl.program_id(2)
    m_ref[...] = jnp.full_like(m_ref, NEG_INF)
    l_ref[...] = jnp.zeros_like(l_ref)
    acc_ref[...] = jnp.zeros_like(acc_ref)
    q = q_ref[...]
    blocks_per_tk = tk // tq
    n_full = qi // blocks_per_tk

    def step(kb, masked):
        k0 = pl.multiple_of(kb * tk, tk)
        s = _dot_nt(q, k_ref[pl.ds(k0, tk), :])
        if masked:
            rpos = qi * tq + lax.broadcasted_iota(jnp.int32, (tq, tk), 0)
            cpos = kb * tk + lax.broadcasted_iota(jnp.int32, (tq, tk), 1)
            s = jnp.where(cpos <= rpos, s, NEG_INF)
        m_prev = m_ref[...]
        m_new = jnp.maximum(m_prev, jnp.max(s, axis=-1, keepdims=True))
        alpha = jnp.exp(m_prev - m_new)
        p = jnp.exp(s - m_new[:, :1])
        l_ref[...] = alpha * l_ref[...] + jnp.sum(p, axis=-1, keepdims=True)
        m_ref[...] = m_new
        acc_ref[...] = acc_ref[...] * alpha + jnp.dot(p.astype(BF16), v_ref[pl.ds(k0, tk), :],
                                                      preferred_element_type=F32)

    def body(kb, carry):
        step(kb, False)
        return carry

    lax.fori_loop(0, n_full, body, 0)
    step(n_full, True)
    o_ref[...] = acc_ref[...] / l_ref[...]


def _mla_prefill(qm, km, vm, batch, seq, tq=256, tk=512):
    nq = seq // tq
    return pl.pallas_call(
        functools.partial(_mla_prefill_kernel, tq=tq, tk=tk),
        grid=(batch, MLA_HEADS, nq),
        in_specs=[pl.BlockSpec((tq, 256), lambda b, h, i: (b * nq + i, h)),
                  pl.BlockSpec((seq, 256), lambda b, h, i: (b, h)),
                  pl.BlockSpec((seq, V_DIM), lambda b, h, i: (b, h))],
        out_specs=pl.BlockSpec((tq, V_DIM), lambda b, h, i: (b * nq + i, h)),
        out_shape=jax.ShapeDtypeStruct((batch * seq, MLA_HEADS * V_DIM), F32),
        scratch_shapes=[pltpu.VMEM((tq, LANES), F32),
                        pltpu.VMEM((tq, LANES), F32),
                        pltpu.VMEM((tq, V_DIM), F32)],
        compiler_params=_cparams(("arbitrary", "arbitrary", "arbitrary"), 32 << 20),
        name="mla_prefill",
    )(qm, km, vm)


def _softmax_update(s, v_bf, m_ref, l_ref, acc_ref):
    m_prev = m_ref[...]
    m_new = jnp.maximum(m_prev, jnp.max(s, axis=-1, keepdims=True))
    alpha = jnp.exp(m_prev - m_new)
    p = jnp.exp(s - m_new[:, :1])
    l_ref[...] = alpha * l_ref[...] + jnp.sum(p, axis=-1, keepdims=True)
    m_ref[...] = m_new
    acc_ref[...] = acc_ref[...] * alpha[:, :1] + jnp.dot(p.astype(BF16), v_bf, preferred_element_type=F32)


def _new_row_mask(nq_rows, n_new, strict):
    pos = lax.shift_right_logical(lax.broadcasted_iota(jnp.int32, (nq_rows, LANES), 0), 3)
    key = lax.broadcasted_iota(jnp.int32, (nq_rows, LANES), 1)
    vis = (key < pos) if strict else (key <= pos)
    return vis & (key < n_new)


def _fox_decode_kernel(pt_ref, q_ref, knew_ref, vnew_ref, lfnew_ref, tri_ref, *refs, n_pages, n_new):
    k_refs = refs[:n_pages]
    v_refs = refs[n_pages:2 * n_pages]
    lf_refs = refs[2 * n_pages:3 * n_pages]
    o_ref, m_ref, l_ref, acc_ref, car_ref = refs[3 * n_pages:]
    c = pl.program_id(1)
    nq_rows = q_ref.shape[0]
    reps = nq_rows // FOX_HEADS

    @pl.when(c == 0)
    def _():
        m_ref[...] = jnp.full_like(m_ref, NEG_INF)
        l_ref[...] = jnp.zeros_like(l_ref)
        acc_ref[...] = jnp.zeros_like(acc_ref)
        car_ref[...] = jnp.zeros_like(car_ref)

    q = q_ref[...]
    tri = tri_ref[...]

    def tile_rows(x):
        return jnp.concatenate([x] * reps, axis=0)

    cum_all = _dot_exact01(jnp.concatenate([r[...] for r in lf_refs], axis=0), tri)
    car = car_ref[:, :1]
    biases = []
    for j in range(n_pages):
        cum = cum_all[FOX_HEADS * j: FOX_HEADS * (j + 1), :] + car
        biases.append(tile_rows(-cum))
        car = cum[:, PAGE_SIZE - 1:PAGE_SIZE]
    car_ref[...] = jnp.broadcast_to(car, car_ref.shape)
    kcat = jnp.concatenate([r[...].astype(BF16) for r in k_refs], axis=0)
    vcat = jnp.concatenate([r[...].astype(BF16) for r in v_refs], axis=0)
    s = _dot_nt(q, kcat) + jnp.concatenate(biases, axis=1)
    _softmax_update(s, vcat, m_ref, l_ref, acc_ref)

    @pl.when(c == pl.num_programs(1) - 1)
    def _():
        cum = _dot_exact01(lfnew_ref[...], tri) + car_ref[:, :1]
        s_new = _dot_nt(q, knew_ref[...].astype(BF16)) + tile_rows(-cum)
        s_new = jnp.where(_new_row_mask(nq_rows, n_new, False), s_new, NEG_INF)
        _softmax_update(s_new, vnew_ref[...].astype(BF16), m_ref, l_ref, acc_ref)
        o_ref[...] = acc_ref[...] / l_ref[:, :1]


def _page_specs(n_pages, pages_per_seq, shape, layer, reverse=False):
    specs = []
    for j in range(n_pages):
        if reverse:
            def imap(b, c, pt, _j=j):
                return (layer, pt[b * pages_per_seq + pages_per_seq - (c + 1) * n_pages + _j], 0, 0)
        else:
            def imap(b, c, pt, _j=j):
                return (layer, pt[b * pages_per_seq + c * n_pages + _j], 0, 0)
        specs.append(pl.BlockSpec((None, None) + tuple(shape), imap))
    return specs


def _fox_decode(pt, qbd, knew, vnew, lfnew, tri, cache_k, cache_v, cache_lf, layer, n_new):
    nb, nq_rows, _ = qbd.shape
    pages_per_seq = pt.shape[0] // nb
    n_pages = PAGES_PER_STEP
    per_b = lambda r, w: pl.BlockSpec((None, r, w), lambda b, c, pt: (b, 0, 0))
    grid_spec = pltpu.PrefetchScalarGridSpec(
        num_scalar_prefetch=1,
        grid=(nb, pages_per_seq // n_pages),
        in_specs=[per_b(nq_rows, LANES),
                  per_b(PAGE_SIZE, LANES), per_b(PAGE_SIZE, LANES), per_b(FOX_HEADS, PAGE_SIZE),
                  pl.BlockSpec((PAGE_SIZE, PAGE_SIZE), lambda b, c, pt: (0, 0))]
                 + _page_specs(n_pages, pages_per_seq, (PAGE_SIZE, LANES), layer)
                 + _page_specs(n_pages, pages_per_seq, (PAGE_SIZE, LANES), layer)
                 + _page_specs(n_pages, pages_per_seq, (FOX_HEADS, PAGE_SIZE), layer),
        out_specs=pl.BlockSpec((None, nq_rows, LANES), lambda b, c, pt: (b, 0, 0)),
        scratch_shapes=[pltpu.VMEM((nq_rows, LANES), F32),
                        pltpu.VMEM((nq_rows, LANES), F32),
                        pltpu.VMEM((nq_rows, LANES), F32),
                        pltpu.VMEM((FOX_HEADS, LANES), F32)],
    )
    return pl.pallas_call(
        functools.partial(_fox_decode_kernel, n_pages=n_pages, n_new=n_new),
        grid_spec=grid_spec,
        out_shape=jax.ShapeDtypeStruct((nb, nq_rows, LANES), F32),
        compiler_params=_cparams(("arbitrary", "arbitrary"), 32 << 20),
        name="fox_decode",
    )(pt, qbd, knew, vnew, lfnew, tri, *([cache_k] * n_pages), *([cache_v] * n_pages),
      *([cache_lf] * n_pages))


def _sb_decode_kernel(pt_ref, q_ref, knew_ref, vnew_ref, urev_ref, *refs, n_pages, n_new):
    k_refs = refs[:n_pages]
    v_refs = refs[n_pages:2 * n_pages]
    o_ref, r_ref, acc_ref = refs[2 * n_pages:]
    c = pl.program_id(1)
    nq_rows = q_ref.shape[0]
    q = q_ref[...]
    urev = urev_ref[...]

    def block(z, keep):
        lr = -(jnp.maximum(z, 0.0) + jnp.log1p(jnp.exp(-jnp.abs(z))))
        if keep is not None:
            lr = jnp.where(keep, lr, 0.0)
        lr1 = lr.astype(BF16)
        lr2 = (lr - lr1.astype(F32)).astype(BF16)
        return (jnp.dot(lr1, urev, preferred_element_type=F32)
                + jnp.dot(lr2, urev, preferred_element_type=F32))

    @pl.when(c == 0)
    def _():
        keep = _new_row_mask(nq_rows, n_new, True)
        z = _dot_nt(q, knew_ref[...].astype(BF16))
        cum = block(z, keep)
        w = jnp.where(keep, jnp.exp(z + cum), 0.0)
        r_ref[...] = jnp.broadcast_to(cum[:, :1], r_ref.shape)
        acc_ref[...] = jnp.dot(w.astype(BF16), vnew_ref[...].astype(BF16), preferred_element_type=F32)

    r = r_ref[:, :1]
    ws = [None] * n_pages
    for j in reversed(range(n_pages)):
        z = _dot_nt(q, k_refs[j][...].astype(BF16))
        cum = block(z, None)
        ws[j] = jnp.exp(z + cum + r).astype(BF16)
        r = r + cum[:, :1]
    r_ref[...] = jnp.broadcast_to(r, r_ref.shape)
    wcat = jnp.concatenate(ws, axis=1)
    vcat = jnp.concatenate([v[...].astype(BF16) for v in v_refs], axis=0)
    acc_ref[...] += jnp.dot(wcat, vcat, preferred_element_type=F32)

    @pl.when(c == pl.num_programs(1) - 1)
    def _():
        o_ref[...] = acc_ref[...]


def _sb_decode(pt, qbd, knew, vnew, urev, cache_k, cache_v, layer, n_new):
    nb, nq_rows, _ = qbd.shape
    pages_per_seq = pt.shape[0] // nb
    n_pages = PAGES_PER_STEP
    per_b = lambda w: pl.BlockSpec((None, PAGE_SIZE, w), lambda b, c, pt: (b, 0, 0))
    grid_spec = pltpu.PrefetchScalarGridSpec(
        num_scalar_prefetch=1,
        grid=(nb, pages_per_seq // n_pages),
        in_specs=[pl.BlockSpec((None, nq_rows, LANES), lambda b, c, pt: (b, 0, 0)),
                  per_b(LANES), per_b(LANES),
                  pl.BlockSpec((PAGE_SIZE, PAGE_SIZE), lambda b, c, pt: (0, 0))]
                 + _page_specs(n_pages, pages_per_seq, (PAGE_SIZE, LANES), layer, reverse=True)
                 + _page_specs(n_pages, pages_per_seq, (PAGE_SIZE, LANES), layer, reverse=True),
        out_specs=pl.BlockSpec((None, nq_rows, LANES), lambda b, c, pt: (b, 0, 0)),
        scratch_shapes=[pltpu.VMEM((nq_rows, LANES), F32),
                        pltpu.VMEM((nq_rows, LANES), F32)],
    )
    return pl.pallas_call(
        functools.partial(_sb_decode_kernel, n_pages=n_pages, n_new=n_new),
        grid_spec=grid_spec,
        out_shape=jax.ShapeDtypeStruct((nb, nq_rows, LANES), F32),
        compiler_params=_cparams(("arbitrary", "arbitrary"), 32 << 20),
        name="sb_decode",
    )(pt, qbd, knew, vnew, urev, *([cache_k] * n_pages), *([cache_v] * n_pages))


def _mla_decode_kernel(pt_ref, ql_ref, qr_ref, cnew_ref, rnew_ref, *refs, n_pages, n_new):
    c_refs = refs[:n_pages]
    r_refs = refs[n_pages:2 * n_pages]
    o_ref, m_ref, l_ref, acc_ref = refs[2 * n_pages:]
    c = pl.program_id(1)
    nq_rows = ql_ref.shape[0]

    @pl.when(c == 0)
    def _():
        m_ref[...] = jnp.full_like(m_ref, NEG_INF)
        l_ref[...] = jnp.zeros_like(l_ref)
        acc_ref[...] = jnp.zeros_like(acc_ref)

    ql = ql_ref[...]
    qr = qr_ref[...]
    ccat = jnp.concatenate([r[...].astype(BF16) for r in c_refs], axis=0)
    rcat = jnp.concatenate([r[...].astype(BF16) for r in r_refs], axis=0)
    s = _dot_nt(ql, ccat) + _dot_nt(qr, rcat)
    _softmax_update(s, ccat, m_ref, l_ref, acc_ref)

    @pl.when(c == pl.num_programs(1) - 1)
    def _():
        cn = cnew_ref[...].astype(BF16)
        s_new = _dot_nt(ql, cn) + _dot_nt(qr, rnew_ref[...].astype(BF16))
        s_new = jnp.where(_new_row_mask(nq_rows, n_new, False), s_new, NEG_INF)
        _softmax_update(s_new, cn, m_ref, l_ref, acc_ref)
        o_ref[...] = acc_ref[...] / l_ref[:, :1]


def _mla_decode(pt, qlat, qrope, cnew, rnew, cache_c, cache_r, layer, n_new):
    nb, nq_rows, _ = qlat.shape
    pages_per_seq = pt.shape[0] // nb
    n_pages = PAGES_PER_STEP
    per_b = lambda w: pl.BlockSpec((None, PAGE_SIZE, w), lambda b, c, pt: (b, 0, 0))
    grid_spec = pltpu.PrefetchScalarGridSpec(
        num_scalar_prefetch=1,
        grid=(nb, pages_per_seq // n_pages),
        in_specs=[pl.BlockSpec((None, nq_rows, KV_LORA), lambda b, c, pt: (b, 0, 0)),
                  pl.BlockSpec((None, nq_rows, ROPE_DIM), lambda b, c, pt: (b, 0, 0)),
                  per_b(KV_LORA), per_b(ROPE_DIM)]
                 + _page_specs(n_pages, pages_per_seq, (PAGE_SIZE, KV_LORA), layer)
                 + _page_specs(n_pages, pages_per_seq, (PAGE_SIZE, ROPE_DIM), layer),
        out_specs=pl.BlockSpec((None, nq_rows, KV_LORA), lambda b, c, pt: (b, 0, 0)),
        scratch_shapes=[pltpu.VMEM((nq_rows, LANES), F32),
                        pltpu.VMEM((nq_rows, LANES), F32),
                        pltpu.VMEM((nq_rows, KV_LORA), F32)],
    )
    return pl.pallas_call(
        functools.partial(_mla_decode_kernel, n_pages=n_pages, n_new=n_new),
        grid_spec=grid_spec,
        out_shape=jax.ShapeDtypeStruct((nb, nq_rows, KV_LORA), F32),
        compiler_params=_cparams(("arbitrary", "arbitrary"), 40 << 20),
        name="mla_decode",
    )(pt, qlat, qrope, cnew, rnew, *([cache_c] * n_pages), *([cache_r] * n_pages))


def _pack_w_in(w_in_l):
    d = w_in_l.shape[0]
    src = {}
    off = 0
    for name, w in (("fq", 512), ("fk", 128), ("fv", 128), ("ff", 8), ("sq", 512), ("sk", 128),
                    ("sv", 128), ("cq", 512), ("ckv", 512), ("kr", 64)):
        src[name] = w_in_l[:, off:off + w]
        off += w
    cols = []
    for name in ("fq", "sq", "cq", "ckv", "fk", "fv", "sk", "sv", "kr", "ff"):
        seg = src[name]
        pad = _SEG[name][1] - seg.shape[1]
        if pad:
            seg = jnp.concatenate([seg, jnp.zeros((d, pad), seg.dtype)], axis=1)
        cols.append(seg)
    return jnp.concatenate(cols, axis=1).astype(BF16)


def _rope_tables(pos):
    half = ROPE_DIM // 2
    inv_freq = ROPE_THETA ** (-jnp.arange(half, dtype=F32) / half)
    ang = pos.astype(F32)[:, None] * inv_freq[None, :]
    cos = jnp.cos(ang)
    sin = jnp.sin(ang)
    return jnp.tile(cos, (1, 4)), jnp.tile(jnp.concatenate([-sin, sin], axis=1), (1, 2))


def _block_diag_queries(q, nb, n_new):
    group = FOX_HEADS // FOX_KV_HEADS
    q = (q * ATTN_SCALE).reshape(nb, n_new, FOX_KV_HEADS, group, HEAD_DIM)
    eye = jnp.eye(FOX_KV_HEADS, dtype=q.dtype)
    qbd = q[:, :, :, :, None, :] * eye[None, None, :, None, :, None]
    return qbd.reshape(nb, n_new * FOX_HEADS, FOX_KV_HEADS * HEAD_DIM).astype(BF16)


def _pad_new_rows(x, nb, n_new):
    w = x.shape[1]
    x = x.reshape(nb, n_new, w)
    return jnp.concatenate([x, jnp.zeros((nb, PAGE_SIZE - n_new, w), x.dtype)], axis=1)


def _heads_from_rows(o, nb, n_new):
    group = FOX_HEADS // FOX_KV_HEADS
    o = o.reshape(nb, n_new, FOX_KV_HEADS, group, FOX_KV_HEADS, HEAD_DIM)
    o = jnp.stack([o[:, :, g, :, g, :] for g in range(FOX_KV_HEADS)], axis=2)
    return o.reshape(nb * n_new, FOX_HEADS * HEAD_DIM)


def kernel(x_prompt, x_sample, cache_fox_k, cache_fox_v, cache_fox_logf, cache_sb_k, cache_sb_v,
           cache_mla_ckv, cache_mla_krope, page_table, w_in, b_f, g_cq, w_uq, g_ckv, w_uk, w_uv,
           g_o, w_o, ln1_g, ln1_b, w_ff1, w_ff2, ln2_g, ln2_b):
    batch, seq, d_model = x_prompt.shape
    nb, n_new, _ = x_sample.shape
    depth = w_in.shape[0]
    n_pool = cache_fox_k.shape[1]
    pages_per_seq = page_table.shape[1]
    past_len = pages_per_seq * PAGE_SIZE
    rows_p = batch * seq
    rows_s = nb * n_new
    assert rows_s == ROW_TILE and seq % ROW_TILE == 0 and pages_per_seq % PAGES_PER_STEP == 0

    x = jnp.concatenate([x_prompt.reshape(rows_p, d_model), x_sample.reshape(rows_s, d_model)], axis=0)
    pos = jnp.concatenate([jnp.tile(jnp.arange(seq), batch), jnp.tile(past_len + jnp.arange(n_new), nb)])
    cos_t, sin_t = _rope_tables(pos)
    pt = page_table.reshape(-1)

    ck = cache_fox_k.reshape(depth, n_pool, PAGE_SIZE, LANES)
    cv = cache_fox_v.reshape(depth, n_pool, PAGE_SIZE, LANES)
    sk = cache_sb_k.reshape(depth, n_pool, PAGE_SIZE, LANES)
    sv = cache_sb_v.reshape(depth, n_pool, PAGE_SIZE, LANES)
    lf_t = jnp.swapaxes(cache_fox_logf, 2, 3)

    r_i = lax.broadcasted_iota(jnp.int32, (ROW_TILE, ROW_TILE), 0)
    c_i = lax.broadcasted_iota(jnp.int32, (ROW_TILE, ROW_TILE), 1)
    tri512 = (r_i <= c_i).astype(BF16)
    tri128 = tri512[:PAGE_SIZE, :PAGE_SIZE]
    urev256 = (r_i >= c_i).astype(BF16)[:256, :256]
    urev128 = urev256[:PAGE_SIZE, :PAGE_SIZE]

    rows_out_p = [[] for _ in range(7)]
    rows_out_s = [[] for _ in range(7)]
    for l in range(depth):
        w_in_p = _pack_w_in(w_in[l])
        wuq3 = w_uq[l].reshape(Q_LORA, MLA_HEADS, NOPE_DIM + ROPE_DIM)
        wuq_p = jnp.concatenate([wuq3[:, :, :NOPE_DIM].reshape(Q_LORA, -1),
                                 wuq3[:, :, NOPE_DIM:].reshape(Q_LORA, -1)], axis=1).astype(BF16)
        wuk_f = w_uk[l].reshape(KV_LORA, MLA_HEADS * NOPE_DIM).astype(BF16)
        wuv_f = w_uv[l].reshape(KV_LORA, MLA_HEADS * V_DIM).astype(BF16)
        wuk_t = jnp.transpose(w_uk[l], (1, 2, 0)).astype(BF16)
        wuv_h = jnp.transpose(w_uv[l], (1, 0, 2)).astype(BF16)
        bf_row = jnp.zeros((1, LANES), F32).at[0, :FOX_HEADS].set(b_f[l])

        u = _proj(x, w_in_p)
        logf, cs, ckvn, krope, qm, km, vm = _prep(
            u, bf_row, g_cq[l][None, :], g_ckv[l][None, :], wuq_p, wuk_f, wuv_f, cos_t, sin_t, tri512,
            seq // ROW_TILE)

        fox_p = _fox_prefill(u, cs, batch, seq)
        sb_p = _sb_prefill(u, urev256, batch, seq)
        mla_p = _mla_prefill(qm, km, vm, batch, seq)

        def seg_s(name, width=None):
            off, w = _SEG[name]
            return u[rows_p:, off:off + (width or w)]

        fox_q = _block_diag_queries(seg_s("fq"), nb, n_new)
        sb_q = _block_diag_queries(seg_s("sq"), nb, n_new)
        fox_s = _fox_decode(pt, fox_q, _pad_new_rows(seg_s("fk"), nb, n_new),
                            _pad_new_rows(seg_s("fv"), nb, n_new),
                            jnp.swapaxes(_pad_new_rows(logf[rows_p:], nb, n_new), 1, 2), tri128,
                            ck, cv, lf_t, l, n_new)
        sb_s = _sb_decode(pt, sb_q, _pad_new_rows(seg_s("sk"), nb, n_new),
                          _pad_new_rows(seg_s("sv"), nb, n_new), urev128, sk, sv, l, n_new)
        qlat = _qlat(qm, wuk_t, rows_p // rows_s, rows_s)
        qlat = jnp.transpose(qlat.reshape(MLA_HEADS, nb, n_new, KV_LORA), (1, 2, 0, 3))
        qlat = qlat.reshape(nb, n_new * MLA_HEADS, KV_LORA)
        qrope = qm[rows_p:].reshape(nb, n_new * MLA_HEADS, 256)[:, :, NOPE_DIM:NOPE_DIM + ROPE_DIM]
        lat = _mla_decode(pt, qlat, qrope, _pad_new_rows(ckvn[rows_p:], nb, n_new),
                          _pad_new_rows(krope[rows_p:], nb, n_new),
                          cache_mla_ckv, cache_mla_krope, l, n_new)
        lat = jnp.transpose(lat.reshape(nb, n_new, MLA_HEADS, KV_LORA), (2, 0, 1, 3))
        mla_s = _latv(lat.reshape(MLA_HEADS, rows_s, KV_LORA), wuv_h)

        fox_o = jnp.concatenate([fox_p, _heads_from_rows(fox_s, nb, n_new)], axis=0)
        sb_o = jnp.concatenate([sb_p, _heads_from_rows(sb_s, nb, n_new)], axis=0)
        mla_o = jnp.concatenate([mla_p, mla_s], axis=0)

        x1, x1b = _outproj(fox_o, sb_o, mla_o, x, g_o[l][None, :], w_o[l].astype(BF16),
                           ln1_g[l][None, :], ln1_b[l][None, :])
        h = _ff1(x1b, w_ff1[l].astype(BF16))
        x = _ff2(h, w_ff2[l].astype(BF16), x1, ln2_g[l][None, :], ln2_b[l][None, :])

        def seg_u(name, width=None):
            off, w = _SEG[name]
            return u[:, off:off + (width or w)]

        leaves = (seg_u("fk").reshape(-1, FOX_KV_HEADS, HEAD_DIM),
                  seg_u("fv").reshape(-1, FOX_KV_HEADS, HEAD_DIM),
                  logf,
                  seg_u("sk").reshape(-1, SB_KV_HEADS, HEAD_DIM),
                  seg_u("sv").reshape(-1, SB_KV_HEADS, HEAD_DIM),
                  ckvn, krope)
        for j, leaf in enumerate(leaves):
            rows_out_p[j].append(leaf[:rows_p].reshape((batch, seq) + leaf.shape[1:]))
            rows_out_s[j].append(leaf[rows_p:].reshape((nb, n_new) + leaf.shape[1:]))

    y_p = x[:rows_p].reshape(batch, seq, d_model)
    y_s = x[rows_p:].reshape(nb, n_new, d_model)
    return (y_p, y_s) + tuple(jnp.stack(r) for r in rows_out_p) + tuple(jnp.stack(r) for r in rows_out_s)
```

```python
import functools

import numpy as np
import jax
import jax.numpy as jnp
from jax import lax
from jax.experimental import pallas as pl
from jax.experimental.pallas import tpu as pltpu

F32 = jnp.float32
BF16 = jnp.bfloat16

HEAD_DIM = 64
FOX_HEADS = 8
FOX_KV_HEADS = 2
SB_HEADS = 8
SB_KV_HEADS = 2
MLA_HEADS = 8
Q_LORA = 512
KV_LORA = 512
NOPE_DIM = 128
ROPE_DIM = 64
V_DIM = 128
PAGE_SIZE = 128
DEPTH = 2
MLA_SCALE = (NOPE_DIM + ROPE_DIM) ** -0.5
ATTN_SCALE = HEAD_DIM ** -0.5
ROPE_THETA = 10000.0
EPS = 1e-5
NEG_INF = -1e30
ALPHA = (2 * DEPTH) ** 0.25

LANES = 128
V7X_VMEM_BYTES = 64 * 1024 * 1024

_SEG = {}
_off = 0
for _name, _w in (("fq", 512), ("sq", 512), ("cq", 512), ("ckv", 512), ("fk", 128), ("fv", 128),
                  ("sk", 128), ("sv", 128), ("kr", 128), ("ff", 128)):
    _SEG[_name] = (_off, _w)
    _off += _w
PACKED_COLS = _off

ROW_TILE = 512
PAGES_PER_STEP = 16
MLA_HEADS_PER_STEP = 4

SB_DONE = -110.0


def _cparams(sem, vmem_bytes):
    return pltpu.CompilerParams(dimension_semantics=sem, vmem_limit_bytes=int(vmem_bytes))


def _log_sigmoid(x):
    return jnp.minimum(x, 0.0) - jnp.log1p(jnp.exp(-jnp.abs(x)))


def _split3(a):
    a1 = a.astype(BF16)
    r1 = a - a1.astype(F32)
    a2 = r1.astype(BF16)
    a3 = (r1 - a2.astype(F32)).astype(BF16)
    return a1, a2, a3


def _dot_exact01(a, m01):
    out = None
    for piece in _split3(a):
        d = jnp.dot(piece, m01, preferred_element_type=F32)
        out = d if out is None else out + d
    return out


def _dot_nt(a, b):
    return lax.dot_general(a, b, (((1,), (1,)), ((), ())), preferred_element_type=F32)


def _lane_iota(shape):
    return lax.broadcasted_iota(jnp.int32, shape, len(shape) - 1)


def _rope_swap(x):
    lane = _lane_iota(x.shape)
    return jnp.where((lane % 64) < 32, pltpu.roll(x, 96, 1), pltpu.roll(x, 32, 1))


def _proj_kernel(x_ref, w_ref, o_ref):
    o_ref[...] = jnp.dot(x_ref[...].astype(BF16), w_ref[...], preferred_element_type=F32)


def _proj(x, w):
    m, k = x.shape
    n = w.shape[1]
    tm = ROW_TILE
    vmem = 2 * (tm * k * 4 + k * n * 2 + tm * n * 4) + tm * k * 2 + tm * n * 4
    return pl.pallas_call(
        _proj_kernel,
        grid=(m // tm,),
        in_specs=[pl.BlockSpec((tm, k), lambda i: (i, 0)),
                  pl.BlockSpec((k, n), lambda i: (0, 0))],
        out_specs=pl.BlockSpec((tm, n), lambda i: (i, 0)),
        out_shape=jax.ShapeDtypeStruct((m, n), F32),
        compiler_params=_cparams(("parallel",), vmem + (4 << 20)),
        name="proj",
    )(x, w)


def _prep_kernel(cq_ref, ckv_ref, kr_ref, ff_ref, bf_ref, gcq_ref, gckv_ref, wuq_ref, wuk_ref,
                 wuv_ref, cos_ref, sin_ref, tri_ref,
                 logf_ref, cs_ref, ckvn_ref, krope_ref, qm_ref, km_ref, vm_ref, car_ref,
                 *, tiles_per_seq):
    i = pl.program_id(0)
    tm = cq_ref.shape[0]
    lane = _lane_iota((tm, LANES))

    lf = _log_sigmoid(ff_ref[...] + bf_ref[...])
    logf_ref[...] = lf[:, :FOX_HEADS]

    @pl.when(i % tiles_per_seq == 0)
    def _():
        car_ref[...] = jnp.zeros_like(car_ref)

    lft = jnp.where(lane < FOX_HEADS, lf, 0.0).T[:FOX_HEADS, :]
    cum = _dot_exact01(lft, tri_ref[...]) + car_ref[:, :1]
    cs_ref[...] = cum
    car_ref[...] = jnp.broadcast_to(cum[:, tm - 1:tm], car_ref.shape)

    cos = cos_ref[...]
    sin = sin_ref[...]

    cq = cq_ref[...]
    cqn = cq * lax.rsqrt(jnp.mean(cq * cq, axis=-1, keepdims=True) + EPS) * gcq_ref[...]
    q = jnp.dot(cqn.astype(BF16), wuq_ref[...], preferred_element_type=F32)
    nope_w = MLA_HEADS * NOPE_DIM
    ropes = []
    for g in range(MLA_HEADS * ROPE_DIM // LANES):
        xg = q[:, nope_w + g * LANES: nope_w + (g + 1) * LANES]
        ropes.append(xg * cos + _rope_swap(xg) * sin)
    for h in range(MLA_HEADS):
        qm_ref[:, 256 * h: 256 * h + 128] = (q[:, 128 * h: 128 * h + 128] * MLA_SCALE).astype(BF16)
        grp = ropes[h // 2]
        if h % 2 == 1:
            grp = pltpu.roll(grp, 64, 1)
        qm_ref[:, 256 * h + 128: 256 * h + 256] = jnp.where(lane < ROPE_DIM, grp * MLA_SCALE, 0.0).astype(BF16)

    ckv = ckv_ref[...]
    ckvn = ckv * lax.rsqrt(jnp.mean(ckv * ckv, axis=-1, keepdims=True) + EPS) * gckv_ref[...]
    ckvn_ref[...] = ckvn
    ckvb = ckvn.astype(BF16)
    knope = jnp.dot(ckvb, wuk_ref[...], preferred_element_type=F32)
    vm_ref[...] = jnp.dot(ckvb, wuv_ref[...], preferred_element_type=F32).astype(BF16)

    kr = kr_ref[...]
    krr = kr * cos + _rope_swap(kr) * sin
    krope_ref[...] = krr[:, :ROPE_DIM]
    kpiece = jnp.where(lane < ROPE_DIM, krr, 0.0).astype(BF16)
    for h in range(MLA_HEADS):
        km_ref[:, 256 * h: 256 * h + 128] = knope[:, 128 * h: 128 * h + 128].astype(BF16)
        km_ref[:, 256 * h + 128: 256 * h + 256] = kpiece


def _prep(u, bf_row, gcq, gckv, wuq, wuk, wuv, cos_t, sin_t, tri, tiles_per_seq):
    m = u.shape[0]
    tm = ROW_TILE
    nt = m // tm

    def seg(name):
        off, w = _SEG[name]
        return pl.BlockSpec((tm, w), lambda i, _c=off // w: (i, _c))

    def const(shape):
        return pl.BlockSpec(shape, lambda i: tuple(0 for _ in shape))

    row = lambda w: pl.BlockSpec((tm, w), lambda i: (i, 0))
    out_shape = (
        jax.ShapeDtypeStruct((m, FOX_HEADS), F32),
        jax.ShapeDtypeStruct((nt, FOX_HEADS, tm), F32),
        jax.ShapeDtypeStruct((m, KV_LORA), F32),
        jax.ShapeDtypeStruct((m, ROPE_DIM), F32),
        jax.ShapeDtypeStruct((m, MLA_HEADS * 256), BF16),
        jax.ShapeDtypeStruct((m, MLA_HEADS * 256), BF16),
        jax.ShapeDtypeStruct((m, MLA_HEADS * V_DIM), BF16),
    )
    out_specs = (
        row(FOX_HEADS),
        pl.BlockSpec((None, FOX_HEADS, tm), lambda i: (i, 0, 0)),
        row(KV_LORA), row(ROPE_DIM), row(MLA_HEADS * 256), row(MLA_HEADS * 256), row(MLA_HEADS * V_DIM),
    )
    return pl.pallas_call(
        functools.partial(_prep_kernel, tiles_per_seq=tiles_per_seq),
        grid=(nt,),
        in_specs=[seg("cq"), seg("ckv"), seg("kr"), seg("ff"),
                  const((1, LANES)), const((1, Q_LORA)), const((1, KV_LORA)),
                  const(wuq.shape), const(wuk.shape), const(wuv.shape),
                  row(LANES), row(LANES), const(tri.shape)],
        out_specs=out_specs,
        out_shape=out_shape,
        scratch_shapes=[pltpu.VMEM((FOX_HEADS, LANES), F32)],
        compiler_params=_cparams(("arbitrary",), 48 << 20),
        name="prep",
    )(u, u, u, u, bf_row, gcq, gckv, wuq, wuk, wuv, cos_t, sin_t, tri)


def _qlat_kernel(q_ref, w_ref, o_ref):
    o_ref[...] = jnp.dot(q_ref[:, :NOPE_DIM], w_ref[...], preferred_element_type=F32).astype(o_ref.dtype)


def _qlat(qm, wukt, row_block, rows):
    return pl.pallas_call(
        _qlat_kernel,
        grid=(MLA_HEADS,),
        in_specs=[pl.BlockSpec((rows, 256), lambda h: (row_block, h)),
                  pl.BlockSpec((None, NOPE_DIM, KV_LORA), lambda h: (h, 0, 0))],
        out_specs=pl.BlockSpec((None, rows, KV_LORA), lambda h: (h, 0, 0)),
        out_shape=jax.ShapeDtypeStruct((MLA_HEADS, rows, KV_LORA), BF16),
        compiler_params=_cparams(("parallel",), 16 << 20),
        name="qlat",
    )(qm, wukt)


def _latv_kernel(lat_ref, w_ref, o_ref):
    o_ref[...] = jnp.dot(lat_ref[...].astype(BF16), w_ref[...], preferred_element_type=F32)


def _latv(lat, wuv_h):
    rows = lat.shape[1]
    return pl.pallas_call(
        _latv_kernel,
        grid=(MLA_HEADS,),
        in_specs=[pl.BlockSpec((None, rows, KV_LORA), lambda h: (h, 0, 0)),
                  pl.BlockSpec((None, KV_LORA, V_DIM), lambda h: (h, 0, 0))],
        out_specs=pl.BlockSpec((rows, V_DIM), lambda h: (0, h)),
        out_shape=jax.ShapeDtypeStruct((rows, MLA_HEADS * V_DIM), F32),
        compiler_params=_cparams(("parallel",), 16 << 20),
        name="latv",
    )(lat, wuv_h)


def _layer_norm(y, g, b):
    mu = jnp.mean(y, axis=-1, keepdims=True)
    d = y - mu
    var = jnp.mean(d * d, axis=-1, keepdims=True)
    return d * lax.rsqrt(var + EPS) * g + b


def _unit_rms(x):
    return x * lax.rsqrt(jnp.mean(x * x, axis=-1, keepdims=True) + EPS)


def _outproj_kernel(fox_ref, sb_ref, mla_ref, x_ref, go_ref, wo_ref, g_ref, b_ref, o_ref, ob_ref):
    go = go_ref[...]
    w0 = FOX_HEADS * HEAD_DIM
    w1 = w0 + SB_HEADS * HEAD_DIM
    parts = ((fox_ref, 0, w0), (sb_ref, w0, w1), (mla_ref, w1, go.shape[1]))
    mix = None
    for ref, lo, hi in parts:
        o = (_unit_rms(ref[...]) * go[:, lo:hi]).astype(BF16)
        d = jnp.dot(o, wo_ref[lo:hi, :], preferred_element_type=F32)
        mix = d if mix is None else mix + d
    y = _layer_norm(ALPHA * x_ref[...] + mix, g_ref[...], b_ref[...])
    o_ref[...] = y
    ob_ref[...] = y.astype(BF16)


def _outproj(fox_o, sb_o, mla_o, x, go, wo, g, b):
    m, d = x.shape
    tm = ROW_TILE
    row = lambda w: pl.BlockSpec((tm, w), lambda i: (i, 0))
    const = lambda shape: pl.BlockSpec(shape, lambda i: (0, 0))
    return pl.pallas_call(
        _outproj_kernel,
        grid=(m // tm,),
        in_specs=[row(fox_o.shape[1]), row(sb_o.shape[1]), row(mla_o.shape[1]), row(d),
                  const((1, d)), const(wo.shape), const((1, d)), const((1, d))],
        out_specs=(row(d), row(d)),
        out_shape=(jax.ShapeDtypeStruct((m, d), F32), jax.ShapeDtypeStruct((m, d), BF16)),
        compiler_params=_cparams(("parallel",), 56 << 20),
        name="outproj",
    )(fox_o, sb_o, mla_o, x, go, wo, g, b)


def _ff1_kernel(x_ref, w_ref, o_ref):
    h = jnp.maximum(jnp.dot(x_ref[...], w_ref[...], preferred_element_type=F32), 0.0)
    o_ref[...] = (h * h).astype(o_ref.dtype)


def _ff1(xb, w1, tn=2048):
    m, k = xb.shape
    n = w1.shape[1]
    tm = ROW_TILE
    return pl.pallas_call(
        _ff1_kernel,
        grid=(n // tn, m // tm),
        in_specs=[pl.BlockSpec((tm, k), lambda j, i: (i, 0)),
                  pl.BlockSpec((k, tn), lambda j, i: (0, j))],
        out_specs=pl.BlockSpec((tm, tn), lambda j, i: (i, j)),
        out_shape=jax.ShapeDtypeStruct((m, n), BF16),
        compiler_params=_cparams(("parallel", "parallel"), 48 << 20),
        name="ff1",
    )(xb, w1)


def _ff2_kernel(h_ref, w_ref, x_ref, g_ref, b_ref, o_ref, acc_ref):
    k = pl.program_id(1)

    @pl.when(k == 0)
    def _():
        acc_ref[...] = jnp.zeros_like(acc_ref)

    acc_ref[...] += jnp.dot(h_ref[...], w_ref[...], preferred_element_type=F32)

    @pl.when(k == pl.num_programs(1) - 1)
    def _():
        o_ref[...] = _layer_norm(ALPHA * x_ref[...] + acc_ref[...], g_ref[...], b_ref[...])


def _ff2(h, w2, x, g, b, tk=2048):
    m, kk = h.shape
    d = w2.shape[1]
    tm = ROW_TILE
    return pl.pallas_call(
        _ff2_kernel,
        grid=(m // tm, kk // tk),
        in_specs=[pl.BlockSpec((tm, tk), lambda i, k: (i, k)),
                  pl.BlockSpec((tk, d), lambda i, k: (k, 0)),
                  pl.BlockSpec((tm, d), lambda i, k: (i, 0)),
                  pl.BlockSpec((1, d), lambda i, k: (0, 0)),
                  pl.BlockSpec((1, d), lambda i, k: (0, 0))],
        out_specs=pl.BlockSpec((tm, d), lambda i, k: (i, 0)),
        out_shape=jax.ShapeDtypeStruct((m, d), F32),
        scratch_shapes=[pltpu.VMEM((tm, d), F32)],
        compiler_params=_cparams(("parallel", "arbitrary"), 48 << 20),
        name="ff2",
    )(h, w2, x, g, b)


def _build_qbd(q_ref, qbd_ref, tq):
    lane = _lane_iota((tq, LANES))
    group = FOX_HEADS // FOX_KV_HEADS
    for h in range(FOX_HEADS):
        g = h // group
        src = q_ref[:, LANES * (h // 2): LANES * (h // 2 + 1)] * ATTN_SCALE
        if (h % 2) != g:
            src = pltpu.roll(src, 64, 1)
        in_half = (lane >= 64 * g) & (lane < 64 * (g + 1))
        qbd_ref[h * tq:(h + 1) * tq, :] = jnp.where(in_half, src, 0.0).astype(BF16)


def _store_heads(o_ref, acc, tq):
    lane = _lane_iota((tq, LANES))
    group = FOX_HEADS // FOX_KV_HEADS
    for pair in range(FOX_HEADS // 2):
        pieces = []
        for h in (2 * pair, 2 * pair + 1):
            x = acc[h * tq:(h + 1) * tq, :]
            if (h % 2) != (h // group):
                x = pltpu.roll(x, 64, 1)
            pieces.append(x)
        o_ref[:, LANES * pair: LANES * (pair + 1)] = jnp.where(lane < 64, pieces[0], pieces[1])


def _fox_prefill_kernel(q_ref, k_ref, v_ref, cs_ref, o_ref,
                        qbd_ref, kb_ref, vb_ref, p_ref, m_ref, l_ref, acc_ref, *, tq, tk):
    qi = pl.program_id(1)
    rows = FOX_HEADS * tq

    @pl.when(qi == 0)
    def _():
        kb_ref[...] = k_ref[...].astype(BF16)
        vb_ref[...] = v_ref[...].astype(BF16)

    _build_qbd(q_ref, qbd_ref, tq)
    m_ref[...] = jnp.full_like(m_ref, NEG_INF)
    l_ref[...] = jnp.zeros_like(l_ref)
    acc_ref[...] = jnp.zeros_like(acc_ref)

    blocks_per_tk = tk // tq
    n_full = qi // blocks_per_tk

    def step(kb, masked):
        k0 = pl.multiple_of(kb * tk, tk)
        s = _dot_nt(qbd_ref[...], kb_ref[pl.ds(k0, tk), :])
        negc = -cs_ref[kb]
        if masked:
            rpos = qi * tq + lax.broadcasted_iota(jnp.int32, (tq, tk), 0)
            cpos = kb * tk + lax.broadcasted_iota(jnp.int32, (tq, tk), 1)
            keep = cpos <= rpos
        for h in range(FOX_HEADS):
            sl = slice(h * tq, (h + 1) * tq)
            sh = s[sl, :] + negc[h:h + 1, :]
            if masked:
                sh = jnp.where(keep, sh, NEG_INF)
            m_prev = m_ref[sl, :]
            m_new = jnp.maximum(m_prev, jnp.max(sh, axis=-1, keepdims=True))
            alpha = jnp.exp(m_prev - m_new)
            p = jnp.exp(sh - jnp.tile(m_new, (1, tk // LANES)))
            l_ref[sl, :] = alpha * l_ref[sl, :] + jnp.sum(p, axis=-1, keepdims=True)
            m_ref[sl, :] = m_new
            acc_ref[sl, :] = acc_ref[sl, :] * alpha
            p_ref[sl, :] = p.astype(BF16)
        acc_ref[...] += jnp.dot(p_ref[...], vb_ref[pl.ds(k0, tk), :], preferred_element_type=F32)

    def body(kb, carry):
        step(kb, False)
        return carry

    lax.fori_loop(0, n_full, body, 0)
    step(n_full, True)

    _store_heads(o_ref, acc_ref[...] / l_ref[...], tq)


def _fox_prefill(u, cs, batch, seq, tq=128):
    tk = ROW_TILE
    nq = seq // tq
    rows = FOX_HEADS * tq
    fq_c = _SEG["fq"][0] // 512
    fk_c = _SEG["fk"][0] // 128
    fv_c = _SEG["fv"][0] // 128
    return pl.pallas_call(
        functools.partial(_fox_prefill_kernel, tq=tq, tk=tk),
        grid=(batch, nq),
        in_specs=[pl.BlockSpec((tq, 512), lambda b, i: (b * nq + i, fq_c)),
                  pl.BlockSpec((seq, 128), lambda b, i: (b, fk_c)),
                  pl.BlockSpec((seq, 128), lambda b, i: (b, fv_c)),
                  pl.BlockSpec((seq // tk, FOX_HEADS, tk), lambda b, i: (b, 0, 0))],
        out_specs=pl.BlockSpec((tq, 512), lambda b, i: (b * nq + i, 0)),
        out_shape=jax.ShapeDtypeStruct((batch * seq, 512), F32),
        scratch_shapes=[pltpu.VMEM((rows, LANES), BF16),
                        pltpu.VMEM((seq, LANES), BF16),
                        pltpu.VMEM((seq, LANES), BF16),
                        pltpu.VMEM((rows, tk), BF16),
                        pltpu.VMEM((rows, LANES), F32),
                        pltpu.VMEM((rows, LANES), F32),
                        pltpu.VMEM((rows, LANES), F32)],
        compiler_params=_cparams(("arbitrary", "arbitrary"), 40 << 20),
        name="fox_prefill",
    )(u, u, u, cs)


def _sb_prefill_kernel(q_ref, k_ref, v_ref, urev_ref, o_ref,
                       qbd_ref, kb_ref, vb_ref, r_ref, acc_ref, *, tq, tk):
    qi = pl.program_id(1)

    @pl.when(qi == 0)
    def _():
        kb_ref[...] = k_ref[...].astype(BF16)
        vb_ref[...] = v_ref[...].astype(BF16)

    _build_qbd(q_ref, qbd_ref, tq)
    r_ref[...] = jnp.zeros_like(r_ref)
    acc_ref[...] = jnp.zeros_like(acc_ref)

    blocks_per_tk = tk // tq
    kb_diag = qi // blocks_per_tk

    def step(kb, masked):
        k0 = pl.multiple_of(kb * tk, tk)
        z = _dot_nt(qbd_ref[...], kb_ref[pl.ds(k0, tk), :])
        lr = -(jnp.maximum(z, 0.0) + jnp.log1p(jnp.exp(-jnp.abs(z))))
        if masked:
            rows = SB_HEADS * tq
            rpos = qi * tq + (lax.broadcasted_iota(jnp.int32, (rows, tk), 0) & (tq - 1))
            cpos = kb * tk + lax.broadcasted_iota(jnp.int32, (rows, tk), 1)
            keep = cpos < rpos
            lr = jnp.where(keep, lr, 0.0)
        lr1 = lr.astype(BF16)
        lr2 = (lr - lr1.astype(F32)).astype(BF16)
        urev = urev_ref[...]
        cum = (jnp.dot(lr1, urev, preferred_element_type=F32)
               + jnp.dot(lr2, urev, preferred_element_type=F32))
        w = jnp.exp(z + cum + r_ref[:, :1])
        if masked:
            w = jnp.where(keep, w, 0.0)
        r_ref[...] += cum[:, :1]
        acc_ref[...] += jnp.dot(w.astype(BF16), vb_ref[pl.ds(k0, tk), :], preferred_element_type=F32)

    step(kb_diag, True)

    def cond(carry):
        j, r_max = carry
        return jnp.logical_and(j < kb_diag, r_max > SB_DONE)

    def body(carry):
        j, _ = carry
        step(kb_diag - 1 - j, False)
        return j + 1, jnp.max(r_ref[...])

    lax.while_loop(cond, body, (jnp.int32(0), jnp.max(r_ref[...])))
    _store_heads(o_ref, acc_ref[...], tq)


def _sb_prefill(u, urev, batch, seq, tq=128):
    tk = urev.shape[0]
    nq = seq // tq
    rows = SB_HEADS * tq
    sq_c = _SEG["sq"][0] // 512
    sk_c = _SEG["sk"][0] // 128
    sv_c = _SEG["sv"][0] // 128
    return pl.pallas_call(
        functools.partial(_sb_prefill_kernel, tq=tq, tk=tk),
        grid=(batch, nq),
        in_specs=[pl.BlockSpec((tq, 512), lambda b, i: (b * nq + i, sq_c)),
                  pl.BlockSpec((seq, 128), lambda b, i: (b, sk_c)),
                  pl.BlockSpec((seq, 128), lambda b, i: (b, sv_c)),
                  pl.BlockSpec((tk, tk), lambda b, i: (0, 0))],
        out_specs=pl.BlockSpec((tq, 512), lambda b, i: (b * nq + i, 0)),
        out_shape=jax.ShapeDtypeStruct((batch * seq, 512), F32),
        scratch_shapes=[pltpu.VMEM((rows, LANES), BF16),
                        pltpu.VMEM((seq, LANES), BF16),
                        pltpu.VMEM((seq, LANES), BF16),
                        pltpu.VMEM((rows, LANES), F32),
                        pltpu.VMEM((rows, LANES), F32)],
        compiler_params=_cparams(("arbitrary", "arbitrary"), 40 << 20),
        name="sb_prefill",
    )(u, u, u, urev)


def _mla_prefill_kernel(q_ref, k_ref, v_ref, o_ref, m_ref, l_ref, acc_ref, *, tq, tk, nh):
    qi = pl.program_id(2)
    m_ref[...] = jnp.full_like(m_ref, NEG_INF)
    l_ref[...] = jnp.zeros_like(l_ref)
    acc_ref[...] = jnp.zeros_like(acc_ref)
    blocks_per_tk = tk // tq
    n_full = qi // blocks_per_tk

    def step(kb, masked):
        k0 = pl.multiple_of(kb * tk, tk)
        if masked:
            rpos = qi * tq + lax.broadcasted_iota(jnp.int32, (tq, tk), 0)
            cpos = kb * tk + lax.broadcasted_iota(jnp.int32, (tq, tk), 1)
            keep = cpos <= rpos
        for h in range(nh):
            sl = slice(h * tq, (h + 1) * tq)
            s = _dot_nt(q_ref[:, 256 * h:256 * (h + 1)], k_ref[pl.ds(k0, tk), 256 * h:256 * (h + 1)])
            if masked:
                s = jnp.where(keep, s, NEG_INF)
            m_prev = m_ref[sl, :]
            m_new = jnp.maximum(m_prev, jnp.max(s, axis=-1, keepdims=True))
            alpha = jnp.exp(m_prev - m_new)
            p = jnp.exp(s - jnp.tile(m_new, (1, tk // LANES)))
            l_ref[sl, :] = alpha * l_ref[sl, :] + jnp.sum(p, axis=-1, keepdims=True)
            m_ref[sl, :] = m_new
            acc_ref[sl, :] = acc_ref[sl, :] * alpha + jnp.dot(
                p.astype(BF16), v_ref[pl.ds(k0, tk), V_DIM * h:V_DIM * (h + 1)], preferred_element_type=F32)

    def body(kb, carry):
        step(kb, False)
        return carry

    lax.fori_loop(0, n_full, body, 0)
    step(n_full, True)
    for h in range(nh):
        sl = slice(h * tq, (h + 1) * tq)
        o_ref[:, V_DIM * h:V_DIM * (h + 1)] = acc_ref[sl, :] / l_ref[sl, :]


def _mla_prefill(qm, km, vm, batch, seq, tq=256, tk=512):
    nq = seq // tq
    nh = MLA_HEADS_PER_STEP
    return pl.pallas_call(
        functools.partial(_mla_prefill_kernel, tq=tq, tk=tk, nh=nh),
        grid=(batch, MLA_HEADS // nh, nq),
        in_specs=[pl.BlockSpec((tq, 256 * nh), lambda b, h, i: (b * nq + i, h)),
                  pl.BlockSpec((seq, 256 * nh), lambda b, h, i: (b, h)),
                  pl.BlockSpec((seq, V_DIM * nh), lambda b, h, i: (b, h))],
        out_specs=pl.BlockSpec((tq, V_DIM * nh), lambda b, h, i: (b * nq + i, h)),
        out_shape=jax.ShapeDtypeStruct((batch * seq, MLA_HEADS * V_DIM), F32),
        scratch_shapes=[pltpu.VMEM((nh * tq, LANES), F32),
                        pltpu.VMEM((nh * tq, LANES), F32),
                        pltpu.VMEM((nh * tq, V_DIM), F32)],
        compiler_params=_cparams(("arbitrary", "arbitrary", "arbitrary"), 48 << 20),
        name="mla_prefill",
    )(qm, km, vm)


def _softmax_update(s, v_bf, m_ref, l_ref, acc_ref, *, v_key_minor):
    m_prev = m_ref[...]
    m_new = jnp.maximum(m_prev, jnp.max(s, axis=-1, keepdims=True))
    alpha = jnp.exp(m_prev - m_new)
    p = jnp.exp(s - jnp.tile(m_new, (1, s.shape[1] // LANES)))
    l_ref[...] = alpha * l_ref[...] + jnp.sum(p, axis=-1, keepdims=True)
    m_ref[...] = m_new
    p = p.astype(BF16)
    pv = _dot_nt(p, v_bf) if v_key_minor else jnp.dot(p, v_bf, preferred_element_type=F32)
    acc_ref[...] = acc_ref[...] * jnp.tile(alpha, (1, acc_ref.shape[1] // LANES)) + pv


def _new_row_mask(nq_rows, n_new, strict):
    pos = lax.shift_right_logical(lax.broadcasted_iota(jnp.int32, (nq_rows, LANES), 0), 3)
    key = lax.broadcasted_iota(jnp.int32, (nq_rows, LANES), 1)
    vis = (key < pos) if strict else (key <= pos)
    return vis & (key < n_new)


def _negc_kernel(pt_ref, lfnew_ref, tri_ref, *refs, n_pages):
    lf_refs = refs[:n_pages]
    o_ref, onew_ref = refs[n_pages:]
    tri = tri_ref[...]
    both = _dot_exact01(jnp.concatenate([r[...] for r in lf_refs], axis=0), tri)
    car = jnp.zeros((FOX_HEADS, PAGE_SIZE), F32)
    for j in range(n_pages):
        rows = slice(FOX_HEADS * j, FOX_HEADS * (j + 1))
        o_ref[:, PAGE_SIZE * j: PAGE_SIZE * (j + 1)] = -(both[rows, :PAGE_SIZE] + car)
        car = car + both[rows, PAGE_SIZE:]
    onew_ref[...] = -(_dot_exact01(lfnew_ref[...], tri)[:, :PAGE_SIZE] + car)


def _negc(pt, lfnew_t, tri, cache_lf_t, layer):
    nb = lfnew_t.shape[0]
    pages_per_seq = pt.shape[0] // nb

    def page_spec(j):
        return pl.BlockSpec((None, None, FOX_HEADS, PAGE_SIZE),
                            lambda b, pt: (layer, pt[b * pages_per_seq + j], 0, 0))

    grid_spec = pltpu.PrefetchScalarGridSpec(
        num_scalar_prefetch=1,
        grid=(nb,),
        in_specs=[pl.BlockSpec((None, FOX_HEADS, PAGE_SIZE), lambda b, pt: (b, 0, 0)),
                  pl.BlockSpec((PAGE_SIZE, 2 * PAGE_SIZE), lambda b, pt: (0, 0))]
                 + [page_spec(j) for j in range(pages_per_seq)],
        out_specs=(pl.BlockSpec((None, FOX_HEADS, pages_per_seq * PAGE_SIZE), lambda b, pt: (b, 0, 0)),
                   pl.BlockSpec((None, FOX_HEADS, PAGE_SIZE), lambda b, pt: (b, 0, 0))),
    )
    return pl.pallas_call(
        functools.partial(_negc_kernel, n_pages=pages_per_seq),
        grid_spec=grid_spec,
        out_shape=(jax.ShapeDtypeStruct((nb, FOX_HEADS, pages_per_seq * PAGE_SIZE), F32),
                   jax.ShapeDtypeStruct((nb, FOX_HEADS, PAGE_SIZE), F32)),
        compiler_params=_cparams(("arbitrary",), 16 << 20),
        name="negc",
    )(pt, lfnew_t, tri, *([cache_lf_t] * pages_per_seq))


def _page_specs(n_pages, pages_per_seq, shape, layer, reverse=False):
    specs = []
    for j in range(n_pages):
        if reverse:
            def imap(b, c, pt, _j=j):
                return (layer, pt[b * pages_per_seq + pages_per_seq - (c + 1) * n_pages + _j], 0, 0)
        else:
            def imap(b, c, pt, _j=j):
                return (layer, pt[b * pages_per_seq + c * n_pages + _j], 0, 0)
        specs.append(pl.BlockSpec((None, None) + tuple(shape), imap))
    return specs


def _decode_kernel(pt_ref, fq_ref, sq_ref, ql_ref, qr_ref, fkn_ref, fvn_ref, negcn_ref, skn_ref, svn_ref,
                   cn_ref, rn_ref, negc_ref, urev_ref, *refs, n_pages, n_new):
    fk, fv, sk, sv, cc, rr = (refs[i * n_pages:(i + 1) * n_pages] for i in range(6))
    fo_ref, so_ref, lo_ref, fm_ref, fl_ref, fa_ref, sr_ref, sa_ref, mm_ref, ml_ref, ma_ref = refs[6 * n_pages:]
    c = pl.program_id(1)
    last = pl.num_programs(1) - 1
    rows = fq_ref.shape[0]
    reps = rows // FOX_HEADS
    fq = fq_ref[...]
    sq = sq_ref[...]
    ql = ql_ref[...]
    qr = qr_ref[...]
    urev = urev_ref[...]

    def cat_lanes(page_refs):
        return jnp.concatenate([r[...].astype(BF16) for r in page_refs], axis=1)

    def tile_rows(x):
        return jnp.concatenate([x] * reps, axis=0)

    def log_rest(z):
        return -(jnp.maximum(z, 0.0) + jnp.log1p(jnp.exp(-jnp.abs(z))))

    def rev_cumsum(lr):
        lr1 = lr.astype(BF16)
        lr2 = (lr - lr1.astype(F32)).astype(BF16)
        return (jnp.dot(lr1, urev, preferred_element_type=F32)
                + jnp.dot(lr2, urev, preferred_element_type=F32))

    @pl.when(c == 0)
    def _():
        for m_ref, l_ref, a_ref in ((fm_ref, fl_ref, fa_ref), (mm_ref, ml_ref, ma_ref)):
            m_ref[...] = jnp.full_like(m_ref, NEG_INF)
            l_ref[...] = jnp.zeros_like(l_ref)
            a_ref[...] = jnp.zeros_like(a_ref)
        keep = _new_row_mask(rows, n_new, True)
        z = jnp.dot(sq, skn_ref[...].astype(BF16), preferred_element_type=F32)
        both = rev_cumsum(jnp.where(keep, log_rest(z), 0.0))
        w = jnp.where(keep, jnp.exp(z + both[:, :PAGE_SIZE]), 0.0)
        sr_ref[...] = both[:, PAGE_SIZE:]
        sa_ref[...] = _dot_nt(w.astype(BF16), svn_ref[...].astype(BF16))

    s = jnp.dot(fq, cat_lanes(fk), preferred_element_type=F32) + tile_rows(negc_ref[...])
    _softmax_update(s, cat_lanes(fv), fm_ref, fl_ref, fa_ref, v_key_minor=True)

    cb = jnp.concatenate([r[...].astype(BF16) for r in cc], axis=0)
    s = _dot_nt(ql, cb) + jnp.dot(qr, cat_lanes(rr), preferred_element_type=F32)
    _softmax_update(s, cb, mm_ref, ml_ref, ma_ref, v_key_minor=False)

    @pl.when(jnp.max(sr_ref[...]) > SB_DONE)
    def _():
        z = jnp.dot(sq, cat_lanes(sk), preferred_element_type=F32)
        lr = log_rest(z)
        both = rev_cumsum(jnp.concatenate(
            [lr[:, PAGE_SIZE * j: PAGE_SIZE * (j + 1)] for j in range(n_pages)], axis=0))
        r = sr_ref[...]
        xs = [None] * n_pages
        for j in reversed(range(n_pages)):
            blk = both[rows * j: rows * (j + 1), :]
            xs[j] = z[:, PAGE_SIZE * j: PAGE_SIZE * (j + 1)] + blk[:, :PAGE_SIZE] + r
            r = r + blk[:, PAGE_SIZE:]
        sr_ref[...] = r
        w = jnp.exp(jnp.concatenate(xs, axis=1)).astype(BF16)
        sa_ref[...] += _dot_nt(w, cat_lanes(sv))

    @pl.when(c == last)
    def _():
        s_new = jnp.dot(fq, fkn_ref[...].astype(BF16), preferred_element_type=F32) + tile_rows(negcn_ref[...])
        s_new = jnp.where(_new_row_mask(rows, n_new, False), s_new, NEG_INF)
        _softmax_update(s_new, fvn_ref[...].astype(BF16), fm_ref, fl_ref, fa_ref, v_key_minor=True)
        fo_ref[...] = fa_ref[...] / fl_ref[...]

        cn = cn_ref[...].astype(BF16)
        s_new = _dot_nt(ql, cn) + jnp.dot(qr, rn_ref[...].astype(BF16), preferred_element_type=F32)
        s_new = jnp.where(_new_row_mask(rows, n_new, False), s_new, NEG_INF)
        _softmax_update(s_new, cn, mm_ref, ml_ref, ma_ref, v_key_minor=False)
        lo_ref[...] = ma_ref[...] / ml_ref[:, :1]

        so_ref[...] = sa_ref[...]


def _decode(pt, fq, sq, ql, qr, fkn, fvn, negcn, skn, svn, cn, rn, negc, urev,
            fk_t, fv_t, sk_t, sv_t, ckv, kr_t, layer, n_new):
    nb, rows, _ = fq.shape
    pages_per_seq = pt.shape[0] // nb
    n_pages = PAGES_PER_STEP
    per_b = lambda r, w: pl.BlockSpec((None, r, w), lambda b, c, pt: (b, 0, 0))
    kv_t = (LANES, PAGE_SIZE)
    grid_spec = pltpu.PrefetchScalarGridSpec(
        num_scalar_prefetch=1,
        grid=(nb, pages_per_seq // n_pages),
        in_specs=[per_b(rows, LANES), per_b(rows, LANES), per_b(rows, KV_LORA), per_b(rows, ROPE_DIM),
                  per_b(*kv_t), per_b(*kv_t), per_b(FOX_HEADS, PAGE_SIZE), per_b(*kv_t), per_b(*kv_t),
                  per_b(PAGE_SIZE, KV_LORA), per_b(ROPE_DIM, PAGE_SIZE),
                  pl.BlockSpec((None, FOX_HEADS, n_pages * PAGE_SIZE), lambda b, c, pt: (b, 0, c)),
                  pl.BlockSpec((PAGE_SIZE, 2 * PAGE_SIZE), lambda b, c, pt: (0, 0))]
                 + _page_specs(n_pages, pages_per_seq, kv_t, layer)
                 + _page_specs(n_pages, pages_per_seq, kv_t, layer)
                 + _page_specs(n_pages, pages_per_seq, kv_t, layer, reverse=True)
                 + _page_specs(n_pages, pages_per_seq, kv_t, layer, reverse=True)
                 + _page_specs(n_pages, pages_per_seq, (PAGE_SIZE, KV_LORA), layer)
                 + _page_specs(n_pages, pages_per_seq, (ROPE_DIM, PAGE_SIZE), layer),
        out_specs=(per_b(rows, LANES), per_b(rows, LANES), per_b(rows, KV_LORA)),
        scratch_shapes=[pltpu.VMEM((rows, LANES), F32), pltpu.VMEM((rows, LANES), F32),
                        pltpu.VMEM((rows, LANES), F32),
                        pltpu.VMEM((rows, LANES), F32), pltpu.VMEM((rows, LANES), F32),
                        pltpu.VMEM((rows, LANES), F32), pltpu.VMEM((rows, LANES), F32),
                        pltpu.VMEM((rows, KV_LORA), F32)],
    )
    return pl.pallas_call(
        functools.partial(_decode_kernel, n_pages=n_pages, n_new=n_new),
        grid_spec=grid_spec,
        out_shape=(jax.ShapeDtypeStruct((nb, rows, LANES), F32),
                   jax.ShapeDtypeStruct((nb, rows, LANES), F32),
                   jax.ShapeDtypeStruct((nb, rows, KV_LORA), F32)),
        compiler_params=_cparams(("arbitrary", "arbitrary"), 40 << 20),
        name="decode",
    )(pt, fq, sq, ql, qr, fkn, fvn, negcn, skn, svn, cn, rn, negc, urev,
      *([fk_t] * n_pages), *([fv_t] * n_pages), *([sk_t] * n_pages), *([sv_t] * n_pages),
      *([ckv] * n_pages), *([kr_t] * n_pages))


def _pack_w_in(w_in_l):
    d = w_in_l.shape[0]
    src = {}
    off = 0
    for name, w in (("fq", 512), ("fk", 128), ("fv", 128), ("ff", 8), ("sq", 512), ("sk", 128),
                    ("sv", 128), ("cq", 512), ("ckv", 512), ("kr", 64)):
        src[name] = w_in_l[:, off:off + w]
        off += w
    cols = []
    for name in ("fq", "sq", "cq", "ckv", "fk", "fv", "sk", "sv", "kr", "ff"):
        seg = src[name]
        pad = _SEG[name][1] - seg.shape[1]
        if pad:
            seg = jnp.concatenate([seg, jnp.zeros((d, pad), seg.dtype)], axis=1)
        cols.append(seg)
    return jnp.concatenate(cols, axis=1).astype(BF16)


def _rope_tables(pos):
    half = ROPE_DIM // 2
    inv_freq = ROPE_THETA ** (-jnp.arange(half, dtype=F32) / half)
    ang = pos.astype(F32)[:, None] * inv_freq[None, :]
    cos = jnp.cos(ang)
    sin = jnp.sin(ang)
    return jnp.tile(cos, (1, 4)), jnp.tile(jnp.concatenate([-sin, sin], axis=1), (1, 2))


def _block_diag_queries(q, nb, n_new):
    group = FOX_HEADS // FOX_KV_HEADS
    q = (q * ATTN_SCALE).reshape(nb, n_new, FOX_KV_HEADS, group, HEAD_DIM)
    eye = jnp.eye(FOX_KV_HEADS, dtype=q.dtype)
    qbd = q[:, :, :, :, None, :] * eye[None, None, :, None, :, None]
    return qbd.reshape(nb, n_new * FOX_HEADS, FOX_KV_HEADS * HEAD_DIM).astype(BF16)


def _pad_new_rows(x, nb, n_new):
    w = x.shape[1]
    x = x.reshape(nb, n_new, w)
    return jnp.concatenate([x, jnp.zeros((nb, PAGE_SIZE - n_new, w), x.dtype)], axis=1)


def _pad_new_rows_t(x, nb, n_new):
    return jnp.swapaxes(_pad_new_rows(x, nb, n_new), 1, 2)


def _pages_key_minor(cache):
    depth, pool, page, kvh, hd = cache.shape
    return jnp.transpose(cache, (0, 1, 3, 4, 2)).reshape(depth, pool, kvh * hd, page)


def _heads_from_rows(o, nb, n_new):
    group = FOX_HEADS // FOX_KV_HEADS
    o = o.reshape(nb, n_new, FOX_KV_HEADS, group, FOX_KV_HEADS, HEAD_DIM)
    o = jnp.stack([o[:, :, g, :, g, :] for g in range(FOX_KV_HEADS)], axis=2)
    return o.reshape(nb * n_new, FOX_HEADS * HEAD_DIM)


def kernel(x_prompt, x_sample, cache_fox_k, cache_fox_v, cache_fox_logf, cache_sb_k, cache_sb_v,
           cache_mla_ckv, cache_mla_krope, page_table, w_in, b_f, g_cq, w_uq, g_ckv, w_uk, w_uv,
           g_o, w_o, ln1_g, ln1_b, w_ff1, w_ff2, ln2_g, ln2_b):
    batch, seq, d_model = x_prompt.shape
    nb, n_new, _ = x_sample.shape
    depth = w_in.shape[0]
    n_pool = cache_fox_k.shape[1]
    pages_per_seq = page_table.shape[1]
    past_len = pages_per_seq * PAGE_SIZE
    rows_p = batch * seq
    rows_s = nb * n_new
    assert rows_s == ROW_TILE and seq % ROW_TILE == 0 and pages_per_seq % PAGES_PER_STEP == 0

    x = jnp.concatenate([x_prompt.reshape(rows_p, d_model), x_sample.reshape(rows_s, d_model)], axis=0)
    pos = jnp.concatenate([jnp.tile(jnp.arange(seq), batch), jnp.tile(past_len + jnp.arange(n_new), nb)])
    cos_t, sin_t = _rope_tables(pos)
    pt = page_table.reshape(-1)

    fk_t = _pages_key_minor(cache_fox_k)
    fv_t = _pages_key_minor(cache_fox_v)
    sk_t = _pages_key_minor(cache_sb_k)
    sv_t = _pages_key_minor(cache_sb_v)
    lf_t = jnp.swapaxes(cache_fox_logf, 2, 3)
    kr_t = jnp.swapaxes(cache_mla_krope, 2, 3)

    r_i = lax.broadcasted_iota(jnp.int32, (ROW_TILE, ROW_TILE), 0)
    c_i = lax.broadcasted_iota(jnp.int32, (ROW_TILE, ROW_TILE), 1)
    tri512 = (r_i <= c_i).astype(BF16)
    urev256 = (r_i >= c_i).astype(BF16)[:256, :256]
    ones128 = jnp.ones((PAGE_SIZE, PAGE_SIZE), BF16)
    tri128 = jnp.concatenate([tri512[:PAGE_SIZE, :PAGE_SIZE], ones128], axis=1)
    urev128 = jnp.concatenate([urev256[:PAGE_SIZE, :PAGE_SIZE], ones128], axis=1)

    rows_out_p = [[] for _ in range(7)]
    rows_out_s = [[] for _ in range(7)]
    for l in range(depth):
        w_in_p = _pack_w_in(w_in[l])
        wuq3 = w_uq[l].reshape(Q_LORA, MLA_HEADS, NOPE_DIM + ROPE_DIM)
        wuq_p = jnp.concatenate([wuq3[:, :, :NOPE_DIM].reshape(Q_LORA, -1),
                                 wuq3[:, :, NOPE_DIM:].reshape(Q_LORA, -1)], axis=1).astype(BF16)
        wuk_f = w_uk[l].reshape(KV_LORA, MLA_HEADS * NOPE_DIM).astype(BF16)
        wuv_f = w_uv[l].reshape(KV_LORA, MLA_HEADS * V_DIM).astype(BF16)
        wuk_t = jnp.transpose(w_uk[l], (1, 2, 0)).astype(BF16)
        wuv_h = jnp.transpose(w_uv[l], (1, 0, 2)).astype(BF16)
        bf_row = jnp.zeros((1, LANES), F32).at[0, :FOX_HEADS].set(b_f[l])

        u = _proj(x, w_in_p)
        logf, cs, ckvn, krope, qm, km, vm = _prep(
            u, bf_row, g_cq[l][None, :], g_ckv[l][None, :], wuq_p, wuk_f, wuv_f, cos_t, sin_t, tri512,
            seq // ROW_TILE)

        fox_p = _fox_prefill(u, cs, batch, seq)
        sb_p = _sb_prefill(u, urev256, batch, seq)
        mla_p = _mla_prefill(qm, km, vm, batch, seq)

        def seg_s(name, width=None):
            off, w = _SEG[name]
            return u[rows_p:, off:off + (width or w)]

        fox_q = _block_diag_queries(seg_s("fq"), nb, n_new)
        sb_q = _block_diag_queries(seg_s("sq"), nb, n_new)
        negc, negc_new = _negc(pt, _pad_new_rows_t(logf[rows_p:], nb, n_new), tri128, lf_t, l)
        qlat = _qlat(qm, wuk_t, rows_p // rows_s, rows_s)
        qlat = jnp.transpose(qlat.reshape(MLA_HEADS, nb, n_new, KV_LORA), (1, 2, 0, 3))
        qlat = qlat.reshape(nb, n_new * MLA_HEADS, KV_LORA)
        qrope = qm[rows_p:].reshape(nb, n_new * MLA_HEADS, 256)[:, :, NOPE_DIM:NOPE_DIM + ROPE_DIM]
        fox_s, sb_s, lat = _decode(
            pt, fox_q, sb_q, qlat, qrope,
            _pad_new_rows_t(seg_s("fk"), nb, n_new), _pad_new_rows_t(seg_s("fv"), nb, n_new), negc_new,
            _pad_new_rows_t(seg_s("sk"), nb, n_new), _pad_new_rows_t(seg_s("sv"), nb, n_new),
            _pad_new_rows(ckvn[rows_p:], nb, n_new), _pad_new_rows_t(krope[rows_p:], nb, n_new),
            negc, urev128, fk_t, fv_t, sk_t, sv_t, cache_mla_ckv, kr_t, l, n_new)
        lat = jnp.transpose(lat.reshape(nb, n_new, MLA_HEADS, KV_LORA), (2, 0, 1, 3))
        mla_s = _latv(lat.reshape(MLA_HEADS, rows_s, KV_LORA), wuv_h)

        fox_o = jnp.concatenate([fox_p, _heads_from_rows(fox_s, nb, n_new)], axis=0)
        sb_o = jnp.concatenate([sb_p, _heads_from_rows(sb_s, nb, n_new)], axis=0)
        mla_o = jnp.concatenate([mla_p, mla_s], axis=0)

        x1, x1b = _outproj(fox_o, sb_o, mla_o, x, g_o[l][None, :], w_o[l].astype(BF16),
                           ln1_g[l][None, :], ln1_b[l][None, :])
        h = _ff1(x1b, w_ff1[l].astype(BF16))
        x = _ff2(h, w_ff2[l].astype(BF16), x1, ln2_g[l][None, :], ln2_b[l][None, :])

        def seg_u(name, width=None):
            off, w = _SEG[name]
            return u[:, off:off + (width or w)]

        leaves = (seg_u("fk").reshape(-1, FOX_KV_HEADS, HEAD_DIM),
                  seg_u("fv").reshape(-1, FOX_KV_HEADS, HEAD_DIM),
                  logf,
                  seg_u("sk").reshape(-1, SB_KV_HEADS, HEAD_DIM),
                  seg_u("sv").reshape(-1, SB_KV_HEADS, HEAD_DIM),
                  ckvn, krope)
        for j, leaf in enumerate(leaves):
            rows_out_p[j].append(leaf[:rows_p].reshape((batch, seq) + leaf.shape[1:]))
            rows_out_s[j].append(leaf[rows_p:].reshape((nb, n_new) + leaf.shape[1:]))

    y_p = x[:rows_p].reshape(batch, seq, d_model)
    y_s = x[rows_p:].reshape(nb, n_new, d_model)
    return (y_p, y_s) + tuple(jnp.stack(r) for r in rows_out_p) + tuple(jnp.stack(r) for r in rows_out_s)
```

```python
import functools

import numpy as np
import jax
import jax.numpy as jnp
from jax import lax
from jax.experimental import pallas as pl
from jax.experimental.pallas import tpu as pltpu

F32 = jnp.float32
BF16 = jnp.bfloat16

HEAD_DIM = 64
FOX_HEADS = 8
FOX_KV_HEADS = 2
SB_HEADS = 8
SB_KV_HEADS = 2
MLA_HEADS = 8
Q_LORA = 512
KV_LORA = 512
NOPE_DIM = 128
ROPE_DIM = 64
V_DIM = 128
PAGE_SIZE = 128
DEPTH = 2
MLA_SCALE = (NOPE_DIM + ROPE_DIM) ** -0.5
ATTN_SCALE = HEAD_DIM ** -0.5
ROPE_THETA = 10000.0
EPS = 1e-5
NEG_INF = -1e30
ALPHA = (2 * DEPTH) ** 0.25

LANES = 128
V7X_VMEM_BYTES = 64 * 1024 * 1024

_SEG = {}
_off = 0
for _name, _w in (("fq", 512), ("sq", 512), ("cq", 512), ("ckv", 512), ("fk", 128), ("fv", 128),
                  ("sk", 128), ("sv", 128), ("kr", 128), ("ff", 128)):
    _SEG[_name] = (_off, _w)
    _off += _w
PACKED_COLS = _off

ROW_TILE = 512
PAGES_PER_STEP = 16
MLA_HEADS_PER_STEP = 4

SB_DONE = -110.0


def _cparams(sem, vmem_bytes):
    return pltpu.CompilerParams(dimension_semantics=sem, vmem_limit_bytes=int(vmem_bytes))


def _log_sigmoid(x):
    return jnp.minimum(x, 0.0) - jnp.log1p(jnp.exp(-jnp.abs(x)))


def _split3(a):
    a1 = a.astype(BF16)
    r1 = a - a1.astype(F32)
    a2 = r1.astype(BF16)
    a3 = (r1 - a2.astype(F32)).astype(BF16)
    return a1, a2, a3


def _dot_exact01(a, m01):
    out = None
    for piece in _split3(a):
        d = jnp.dot(piece, m01, preferred_element_type=F32)
        out = d if out is None else out + d
    return out


def _dot_nt(a, b):
    return lax.dot_general(a, b, (((1,), (1,)), ((), ())), preferred_element_type=F32)


def _lane_iota(shape):
    return lax.broadcasted_iota(jnp.int32, shape, len(shape) - 1)


def _rope_swap(x):
    lane = _lane_iota(x.shape)
    return jnp.where((lane % 64) < 32, pltpu.roll(x, 96, 1), pltpu.roll(x, 32, 1))


def _proj_kernel(x_ref, w_ref, o_ref):
    o_ref[...] = jnp.dot(x_ref[...].astype(BF16), w_ref[...], preferred_element_type=F32)


def _proj(x, w):
    m, k = x.shape
    n = w.shape[1]
    tm = ROW_TILE
    vmem = 2 * (tm * k * 4 + k * n * 2 + tm * n * 4) + tm * k * 2 + tm * n * 4
    return pl.pallas_call(
        _proj_kernel,
        grid=(m // tm,),
        in_specs=[pl.BlockSpec((tm, k), lambda i: (i, 0)),
                  pl.BlockSpec((k, n), lambda i: (0, 0))],
        out_specs=pl.BlockSpec((tm, n), lambda i: (i, 0)),
        out_shape=jax.ShapeDtypeStruct((m, n), F32),
        compiler_params=_cparams(("parallel",), vmem + (4 << 20)),
        name="proj",
    )(x, w)


def _prep_kernel(cq_ref, ckv_ref, kr_ref, ff_ref, bf_ref, gcq_ref, gckv_ref, wuq_ref, wuk_ref,
                 wuv_ref, cos_ref, sin_ref, tri_ref,
                 logf_ref, cs_ref, ckvn_ref, krope_ref, qm_ref, km_ref, vm_ref, car_ref,
                 *, tiles_per_seq):
    i = pl.program_id(0)
    tm = cq_ref.shape[0]
    lane = _lane_iota((tm, LANES))

    lf = _log_sigmoid(ff_ref[...] + bf_ref[...])
    logf_ref[...] = lf[:, :FOX_HEADS]

    @pl.when(i % tiles_per_seq == 0)
    def _():
        car_ref[...] = jnp.zeros_like(car_ref)

    lft = jnp.where(lane < FOX_HEADS, lf, 0.0).T[:FOX_HEADS, :]
    cum = _dot_exact01(lft, tri_ref[...]) + car_ref[:, :1]
    cs_ref[...] = cum
    car_ref[...] = jnp.broadcast_to(cum[:, tm - 1:tm], car_ref.shape)

    cos = cos_ref[...]
    sin = sin_ref[...]

    cq = cq_ref[...]
    cqn = cq * lax.rsqrt(jnp.mean(cq * cq, axis=-1, keepdims=True) + EPS) * gcq_ref[...]
    q = jnp.dot(cqn.astype(BF16), wuq_ref[...], preferred_element_type=F32)
    nope_w = MLA_HEADS * NOPE_DIM
    ropes = []
    for g in range(MLA_HEADS * ROPE_DIM // LANES):
        xg = q[:, nope_w + g * LANES: nope_w + (g + 1) * LANES]
        ropes.append(xg * cos + _rope_swap(xg) * sin)
    for h in range(MLA_HEADS):
        qm_ref[:, 256 * h: 256 * h + 128] = (q[:, 128 * h: 128 * h + 128] * MLA_SCALE).astype(BF16)
        grp = ropes[h // 2]
        if h % 2 == 1:
            grp = pltpu.roll(grp, 64, 1)
        qm_ref[:, 256 * h + 128: 256 * h + 256] = jnp.where(lane < ROPE_DIM, grp * MLA_SCALE, 0.0).astype(BF16)

    ckv = ckv_ref[...]
    ckvn = ckv * lax.rsqrt(jnp.mean(ckv * ckv, axis=-1, keepdims=True) + EPS) * gckv_ref[...]
    ckvn_ref[...] = ckvn
    ckvb = ckvn.astype(BF16)
    knope = jnp.dot(ckvb, wuk_ref[...], preferred_element_type=F32)
    vm_ref[...] = jnp.dot(ckvb, wuv_ref[...], preferred_element_type=F32).astype(BF16)

    kr = kr_ref[...]
    krr = kr * cos + _rope_swap(kr) * sin
    krope_ref[...] = krr[:, :ROPE_DIM]
    kpiece = jnp.where(lane < ROPE_DIM, krr, 0.0).astype(BF16)
    for h in range(MLA_HEADS):
        km_ref[:, 256 * h: 256 * h + 128] = knope[:, 128 * h: 128 * h + 128].astype(BF16)
        km_ref[:, 256 * h + 128: 256 * h + 256] = kpiece


def _prep(u, bf_row, gcq, gckv, wuq, wuk, wuv, cos_t, sin_t, tri, tiles_per_seq):
    m = u.shape[0]
    tm = ROW_TILE
    nt = m // tm

    def seg(name):
        off, w = _SEG[name]
        return pl.BlockSpec((tm, w), lambda i, _c=off // w: (i, _c))

    def const(shape):
        return pl.BlockSpec(shape, lambda i: tuple(0 for _ in shape))

    row = lambda w: pl.BlockSpec((tm, w), lambda i: (i, 0))
    out_shape = (
        jax.ShapeDtypeStruct((m, FOX_HEADS), F32),
        jax.ShapeDtypeStruct((nt, FOX_HEADS, tm), F32),
        jax.ShapeDtypeStruct((m, KV_LORA), F32),
        jax.ShapeDtypeStruct((m, ROPE_DIM), F32),
        jax.ShapeDtypeStruct((m, MLA_HEADS * 256), BF16),
        jax.ShapeDtypeStruct((m, MLA_HEADS * 256), BF16),
        jax.ShapeDtypeStruct((m, MLA_HEADS * V_DIM), BF16),
    )
    out_specs = (
        row(FOX_HEADS),
        pl.BlockSpec((None, FOX_HEADS, tm), lambda i: (i, 0, 0)),
        row(KV_LORA), row(ROPE_DIM), row(MLA_HEADS * 256), row(MLA_HEADS * 256), row(MLA_HEADS * V_DIM),
    )
    return pl.pallas_call(
        functools.partial(_prep_kernel, tiles_per_seq=tiles_per_seq),
        grid=(nt,),
        in_specs=[seg("cq"), seg("ckv"), seg("kr"), seg("ff"),
                  const((1, LANES)), const((1, Q_LORA)), const((1, KV_LORA)),
                  const(wuq.shape), const(wuk.shape), const(wuv.shape),
                  row(LANES), row(LANES), const(tri.shape)],
        out_specs=out_specs,
        out_shape=out_shape,
        scratch_shapes=[pltpu.VMEM((FOX_HEADS, LANES), F32)],
        compiler_params=_cparams(("arbitrary",), 48 << 20),
        name="prep",
    )(u, u, u, u, bf_row, gcq, gckv, wuq, wuk, wuv, cos_t, sin_t, tri)


def _qlat_kernel(q_ref, w_ref, o_ref):
    o_ref[...] = jnp.dot(q_ref[:, :NOPE_DIM], w_ref[...], preferred_element_type=F32).astype(o_ref.dtype)


def _qlat(qm, wukt, row_block, rows):
    return pl.pallas_call(
        _qlat_kernel,
        grid=(MLA_HEADS,),
        in_specs=[pl.BlockSpec((rows, 256), lambda h: (row_block, h)),
                  pl.BlockSpec((None, NOPE_DIM, KV_LORA), lambda h: (h, 0, 0))],
        out_specs=pl.BlockSpec((None, rows, KV_LORA), lambda h: (h, 0, 0)),
        out_shape=jax.ShapeDtypeStruct((MLA_HEADS, rows, KV_LORA), BF16),
        compiler_params=_cparams(("parallel",), 16 << 20),
        name="qlat",
    )(qm, wukt)


def _latv_kernel(lat_ref, w_ref, o_ref):
    o_ref[...] = jnp.dot(lat_ref[...].astype(BF16), w_ref[...], preferred_element_type=F32)


def _latv(lat, wuv_h):
    rows = lat.shape[1]
    return pl.pallas_call(
        _latv_kernel,
        grid=(MLA_HEADS,),
        in_specs=[pl.BlockSpec((None, rows, KV_LORA), lambda h: (h, 0, 0)),
                  pl.BlockSpec((None, KV_LORA, V_DIM), lambda h: (h, 0, 0))],
        out_specs=pl.BlockSpec((rows, V_DIM), lambda h: (0, h)),
        out_shape=jax.ShapeDtypeStruct((rows, MLA_HEADS * V_DIM), F32),
        compiler_params=_cparams(("parallel",), 16 << 20),
        name="latv",
    )(lat, wuv_h)


def _layer_norm(y, g, b):
    mu = jnp.mean(y, axis=-1, keepdims=True)
    d = y - mu
    var = jnp.mean(d * d, axis=-1, keepdims=True)
    return d * lax.rsqrt(var + EPS) * g + b


def _unit_rms(x):
    return x * lax.rsqrt(jnp.mean(x * x, axis=-1, keepdims=True) + EPS)


def _outproj_kernel(fox_ref, sb_ref, mla_ref, x_ref, go_ref, wo_ref, g_ref, b_ref, o_ref, ob_ref):
    go = go_ref[...]
    w0 = FOX_HEADS * HEAD_DIM
    w1 = w0 + SB_HEADS * HEAD_DIM
    parts = ((fox_ref, 0, w0), (sb_ref, w0, w1), (mla_ref, w1, go.shape[1]))
    mix = None
    for ref, lo, hi in parts:
        o = (_unit_rms(ref[...]) * go[:, lo:hi]).astype(BF16)
        d = jnp.dot(o, wo_ref[lo:hi, :], preferred_element_type=F32)
        mix = d if mix is None else mix + d
    y = _layer_norm(ALPHA * x_ref[...] + mix, g_ref[...], b_ref[...])
    o_ref[...] = y
    ob_ref[...] = y.astype(BF16)


def _outproj(fox_o, sb_o, mla_o, x, go, wo, g, b):
    m, d = x.shape
    tm = ROW_TILE
    row = lambda w: pl.BlockSpec((tm, w), lambda i: (i, 0))
    const = lambda shape: pl.BlockSpec(shape, lambda i: (0, 0))
    return pl.pallas_call(
        _outproj_kernel,
        grid=(m // tm,),
        in_specs=[row(fox_o.shape[1]), row(sb_o.shape[1]), row(mla_o.shape[1]), row(d),
                  const((1, d)), const(wo.shape), const((1, d)), const((1, d))],
        out_specs=(row(d), row(d)),
        out_shape=(jax.ShapeDtypeStruct((m, d), F32), jax.ShapeDtypeStruct((m, d), BF16)),
        compiler_params=_cparams(("parallel",), 56 << 20),
        name="outproj",
    )(fox_o, sb_o, mla_o, x, go, wo, g, b)


def _ff1_kernel(x_ref, w_ref, o_ref):
    h = jnp.maximum(jnp.dot(x_ref[...], w_ref[...], preferred_element_type=F32), 0.0)
    o_ref[...] = (h * h).astype(o_ref.dtype)


def _ff1(xb, w1, tn=2048):
    m, k = xb.shape
    n = w1.shape[1]
    tm = ROW_TILE
    return pl.pallas_call(
        _ff1_kernel,
        grid=(n // tn, m // tm),
        in_specs=[pl.BlockSpec((tm, k), lambda j, i: (i, 0)),
                  pl.BlockSpec((k, tn), lambda j, i: (0, j))],
        out_specs=pl.BlockSpec((tm, tn), lambda j, i: (i, j)),
        out_shape=jax.ShapeDtypeStruct((m, n), BF16),
        compiler_params=_cparams(("parallel", "parallel"), 48 << 20),
        name="ff1",
    )(xb, w1)


def _ff2_kernel(h_ref, w_ref, x_ref, g_ref, b_ref, o_ref, acc_ref):
    k = pl.program_id(1)

    @pl.when(k == 0)
    def _():
        acc_ref[...] = jnp.zeros_like(acc_ref)

    acc_ref[...] += jnp.dot(h_ref[...], w_ref[...], preferred_element_type=F32)

    @pl.when(k == pl.num_programs(1) - 1)
    def _():
        o_ref[...] = _layer_norm(ALPHA * x_ref[...] + acc_ref[...], g_ref[...], b_ref[...])


def _ff2(h, w2, x, g, b, tk=2048):
    m, kk = h.shape
    d = w2.shape[1]
    tm = ROW_TILE
    return pl.pallas_call(
        _ff2_kernel,
        grid=(m // tm, kk // tk),
        in_specs=[pl.BlockSpec((tm, tk), lambda i, k: (i, k)),
                  pl.BlockSpec((tk, d), lambda i, k: (k, 0)),
                  pl.BlockSpec((tm, d), lambda i, k: (i, 0)),
                  pl.BlockSpec((1, d), lambda i, k: (0, 0)),
                  pl.BlockSpec((1, d), lambda i, k: (0, 0))],
        out_specs=pl.BlockSpec((tm, d), lambda i, k: (i, 0)),
        out_shape=jax.ShapeDtypeStruct((m, d), F32),
        scratch_shapes=[pltpu.VMEM((tm, d), F32)],
        compiler_params=_cparams(("parallel", "arbitrary"), 48 << 20),
        name="ff2",
    )(h, w2, x, g, b)


def _build_qbd(q_ref, qbd_ref, tq):
    lane = _lane_iota((tq, LANES))
    group = FOX_HEADS // FOX_KV_HEADS
    for h in range(FOX_HEADS):
        g = h // group
        src = q_ref[:, LANES * (h // 2): LANES * (h // 2 + 1)] * ATTN_SCALE
        if (h % 2) != g:
            src = pltpu.roll(src, 64, 1)
        in_half = (lane >= 64 * g) & (lane < 64 * (g + 1))
        qbd_ref[h * tq:(h + 1) * tq, :] = jnp.where(in_half, src, 0.0).astype(BF16)


def _store_heads(o_ref, acc, tq):
    lane = _lane_iota((tq, LANES))
    group = FOX_HEADS // FOX_KV_HEADS
    for pair in range(FOX_HEADS // 2):
        pieces = []
        for h in (2 * pair, 2 * pair + 1):
            x = acc[h * tq:(h + 1) * tq, :]
            if (h % 2) != (h // group):
                x = pltpu.roll(x, 64, 1)
            pieces.append(x)
        o_ref[:, LANES * pair: LANES * (pair + 1)] = jnp.where(lane < 64, pieces[0], pieces[1])


def _fox_prefill_kernel(q_ref, k_ref, v_ref, cs_ref, o_ref,
                        qbd_ref, kb_ref, vb_ref, p_ref, m_ref, l_ref, acc_ref, *, tq, tk):
    qi = pl.program_id(1)
    rows = FOX_HEADS * tq

    @pl.when(qi == 0)
    def _():
        kb_ref[...] = k_ref[...].astype(BF16)
        vb_ref[...] = v_ref[...].astype(BF16)

    _build_qbd(q_ref, qbd_ref, tq)
    m_ref[...] = jnp.full_like(m_ref, NEG_INF)
    l_ref[...] = jnp.zeros_like(l_ref)
    acc_ref[...] = jnp.zeros_like(acc_ref)

    blocks_per_tk = tk // tq
    n_full = qi // blocks_per_tk

    def step(kb, masked):
        k0 = pl.multiple_of(kb * tk, tk)
        s = _dot_nt(qbd_ref[...], kb_ref[pl.ds(k0, tk), :])
        negc = -cs_ref[kb]
        if masked:
            rpos = qi * tq + lax.broadcasted_iota(jnp.int32, (tq, tk), 0)
            cpos = kb * tk + lax.broadcasted_iota(jnp.int32, (tq, tk), 1)
            keep = cpos <= rpos
        for h in range(FOX_HEADS):
            sl = slice(h * tq, (h + 1) * tq)
            sh = s[sl, :] + negc[h:h + 1, :]
            if masked:
                sh = jnp.where(keep, sh, NEG_INF)
            m_prev = m_ref[sl, :]
            m_new = jnp.maximum(m_prev, jnp.max(sh, axis=-1, keepdims=True))
            alpha = jnp.exp(m_prev - m_new)
            p = jnp.exp(sh - jnp.tile(m_new, (1, tk // LANES)))
            l_ref[sl, :] = alpha * l_ref[sl, :] + jnp.sum(p, axis=-1, keepdims=True)
            m_ref[sl, :] = m_new
            acc_ref[sl, :] = acc_ref[sl, :] * alpha
            p_ref[sl, :] = p.astype(BF16)
        acc_ref[...] += jnp.dot(p_ref[...], vb_ref[pl.ds(k0, tk), :], preferred_element_type=F32)

    def body(kb, carry):
        step(kb, False)
        return carry

    lax.fori_loop(0, n_full, body, 0)
    step(n_full, True)

    _store_heads(o_ref, acc_ref[...] / l_ref[...], tq)


def _fox_prefill(u, cs, batch, seq, tq=256):
    tk = ROW_TILE
    nq = seq // tq
    rows = FOX_HEADS * tq
    fq_c = _SEG["fq"][0] // 512
    fk_c = _SEG["fk"][0] // 128
    fv_c = _SEG["fv"][0] // 128
    return pl.pallas_call(
        functools.partial(_fox_prefill_kernel, tq=tq, tk=tk),
        grid=(batch, nq),
        in_specs=[pl.BlockSpec((tq, 512), lambda b, i: (b * nq + i, fq_c)),
                  pl.BlockSpec((seq, 128), lambda b, i: (b, fk_c)),
                  pl.BlockSpec((seq, 128), lambda b, i: (b, fv_c)),
                  pl.BlockSpec((seq // tk, FOX_HEADS, tk), lambda b, i: (b, 0, 0))],
        out_specs=pl.BlockSpec((tq, 512), lambda b, i: (b * nq + i, 0)),
        out_shape=jax.ShapeDtypeStruct((batch * seq, 512), F32),
        scratch_shapes=[pltpu.VMEM((rows, LANES), BF16),
                        pltpu.VMEM((seq, LANES), BF16),
                        pltpu.VMEM((seq, LANES), BF16),
                        pltpu.VMEM((rows, tk), BF16),
                        pltpu.VMEM((rows, LANES), F32),
                        pltpu.VMEM((rows, LANES), F32),
                        pltpu.VMEM((rows, LANES), F32)],
        compiler_params=_cparams(("arbitrary", "arbitrary"), 40 << 20),
        name="fox_prefill",
    )(u, u, u, cs)


def _sb_prefill_kernel(q_ref, k_ref, v_ref, urev_ref, o_ref,
                       qbd_ref, kb_ref, vb_ref, r_ref, acc_ref, *, tq, tk):
    qi = pl.program_id(1)

    @pl.when(qi == 0)
    def _():
        kb_ref[...] = k_ref[...].astype(BF16)
        vb_ref[...] = v_ref[...].astype(BF16)

    _build_qbd(q_ref, qbd_ref, tq)
    r_ref[...] = jnp.zeros_like(r_ref)
    acc_ref[...] = jnp.zeros_like(acc_ref)

    blocks_per_tk = tk // tq
    kb_diag = qi // blocks_per_tk

    def step(kb, masked):
        k0 = pl.multiple_of(kb * tk, tk)
        z = _dot_nt(qbd_ref[...], kb_ref[pl.ds(k0, tk), :])
        lr = -(jnp.maximum(z, 0.0) + jnp.log1p(jnp.exp(-jnp.abs(z))))
        if masked:
            rows = SB_HEADS * tq
            rpos = qi * tq + (lax.broadcasted_iota(jnp.int32, (rows, tk), 0) & (tq - 1))
            cpos = kb * tk + lax.broadcasted_iota(jnp.int32, (rows, tk), 1)
            keep = cpos < rpos
            lr = jnp.where(keep, lr, 0.0)
        lr1 = lr.astype(BF16)
        lr2 = (lr - lr1.astype(F32)).astype(BF16)
        urev = urev_ref[...]
        cum = (jnp.dot(lr1, urev, preferred_element_type=F32)
               + jnp.dot(lr2, urev, preferred_element_type=F32))
        w = jnp.exp(z + cum + r_ref[:, :1])
        if masked:
            w = jnp.where(keep, w, 0.0)
        r_ref[...] += cum[:, :1]
        acc_ref[...] += jnp.dot(w.astype(BF16), vb_ref[pl.ds(k0, tk), :], preferred_element_type=F32)

    step(kb_diag, True)

    def cond(carry):
        j, r_max = carry
        return jnp.logical_and(j < kb_diag, r_max > SB_DONE)

    def body(carry):
        j, _ = carry
        step(kb_diag - 1 - j, False)
        return j + 1, jnp.max(r_ref[...])

    lax.while_loop(cond, body, (jnp.int32(0), jnp.max(r_ref[...])))
    _store_heads(o_ref, acc_ref[...], tq)


def _sb_prefill(u, urev, batch, seq, tq=128):
    tk = urev.shape[0]
    nq = seq // tq
    rows = SB_HEADS * tq
    sq_c = _SEG["sq"][0] // 512
    sk_c = _SEG["sk"][0] // 128
    sv_c = _SEG["sv"][0] // 128
    return pl.pallas_call(
        functools.partial(_sb_prefill_kernel, tq=tq, tk=tk),
        grid=(batch, nq),
        in_specs=[pl.BlockSpec((tq, 512), lambda b, i: (b * nq + i, sq_c)),
                  pl.BlockSpec((seq, 128), lambda b, i: (b, sk_c)),
                  pl.BlockSpec((seq, 128), lambda b, i: (b, sv_c)),
                  pl.BlockSpec((tk, tk), lambda b, i: (0, 0))],
        out_specs=pl.BlockSpec((tq, 512), lambda b, i: (b * nq + i, 0)),
        out_shape=jax.ShapeDtypeStruct((batch * seq, 512), F32),
        scratch_shapes=[pltpu.VMEM((rows, LANES), BF16),
                        pltpu.VMEM((seq, LANES), BF16),
                        pltpu.VMEM((seq, LANES), BF16),
                        pltpu.VMEM((rows, LANES), F32),
                        pltpu.VMEM((rows, LANES), F32)],
        compiler_params=_cparams(("arbitrary", "arbitrary"), 40 << 20),
        name="sb_prefill",
    )(u, u, u, urev)


def _mla_prefill_kernel(q_ref, k_ref, v_ref, o_ref, m_ref, l_ref, acc_ref, *, tq, tk, nh):
    qi = pl.program_id(2)
    m_ref[...] = jnp.full_like(m_ref, NEG_INF)
    l_ref[...] = jnp.zeros_like(l_ref)
    acc_ref[...] = jnp.zeros_like(acc_ref)
    blocks_per_tk = tk // tq
    n_full = qi // blocks_per_tk

    def step(kb, masked):
        k0 = pl.multiple_of(kb * tk, tk)
        if masked:
            rpos = qi * tq + lax.broadcasted_iota(jnp.int32, (tq, tk), 0)
            cpos = kb * tk + lax.broadcasted_iota(jnp.int32, (tq, tk), 1)
            keep = cpos <= rpos
        for h in range(nh):
            sl = slice(h * tq, (h + 1) * tq)
            s = _dot_nt(q_ref[:, 256 * h:256 * (h + 1)], k_ref[pl.ds(k0, tk), 256 * h:256 * (h + 1)])
            if masked:
                s = jnp.where(keep, s, NEG_INF)
            m_prev = m_ref[sl, :]
            m_new = jnp.maximum(m_prev, jnp.max(s, axis=-1, keepdims=True))
            alpha = jnp.exp(m_prev - m_new)
            p = jnp.exp(s - jnp.tile(m_new, (1, tk // LANES)))
            l_ref[sl, :] = alpha * l_ref[sl, :] + jnp.sum(p, axis=-1, keepdims=True)
            m_ref[sl, :] = m_new
            acc_ref[sl, :] = acc_ref[sl, :] * alpha + jnp.dot(
                p.astype(BF16), v_ref[pl.ds(k0, tk), V_DIM * h:V_DIM * (h + 1)], preferred_element_type=F32)

    def body(kb, carry):
        step(kb, False)
        return carry

    lax.fori_loop(0, n_full, body, 0)
    step(n_full, True)
    for h in range(nh):
        sl = slice(h * tq, (h + 1) * tq)
        o_ref[:, V_DIM * h:V_DIM * (h + 1)] = acc_ref[sl, :] / l_ref[sl, :]


def _mla_prefill(qm, km, vm, batch, seq, tq=512, tk=512):
    nq = seq // tq
    nh = MLA_HEADS_PER_STEP
    return pl.pallas_call(
        functools.partial(_mla_prefill_kernel, tq=tq, tk=tk, nh=nh),
        grid=(batch, MLA_HEADS // nh, nq),
        in_specs=[pl.BlockSpec((tq, 256 * nh), lambda b, h, i: (b * nq + i, h)),
                  pl.BlockSpec((seq, 256 * nh), lambda b, h, i: (b, h)),
                  pl.BlockSpec((seq, V_DIM * nh), lambda b, h, i: (b, h))],
        out_specs=pl.BlockSpec((tq, V_DIM * nh), lambda b, h, i: (b * nq + i, h)),
        out_shape=jax.ShapeDtypeStruct((batch * seq, MLA_HEADS * V_DIM), F32),
        scratch_shapes=[pltpu.VMEM((nh * tq, LANES), F32),
                        pltpu.VMEM((nh * tq, LANES), F32),
                        pltpu.VMEM((nh * tq, V_DIM), F32)],
        compiler_params=_cparams(("arbitrary", "arbitrary", "arbitrary"), 48 << 20),
        name="mla_prefill",
    )(qm, km, vm)


def _softmax_update(s, v_bf, m_ref, l_ref, acc_ref, *, v_key_minor):
    m_prev = m_ref[...]
    m_new = jnp.maximum(m_prev, jnp.max(s, axis=-1, keepdims=True))
    alpha = jnp.exp(m_prev - m_new)
    p = jnp.exp(s - jnp.tile(m_new, (1, s.shape[1] // LANES)))
    l_ref[...] = alpha * l_ref[...] + jnp.sum(p, axis=-1, keepdims=True)
    m_ref[...] = m_new
    p = p.astype(BF16)
    pv = _dot_nt(p, v_bf) if v_key_minor else jnp.dot(p, v_bf, preferred_element_type=F32)
    acc_ref[...] = acc_ref[...] * jnp.tile(alpha, (1, acc_ref.shape[1] // LANES)) + pv


def _new_row_mask(nq_rows, n_new, strict):
    pos = lax.shift_right_logical(lax.broadcasted_iota(jnp.int32, (nq_rows, LANES), 0), 3)
    key = lax.broadcasted_iota(jnp.int32, (nq_rows, LANES), 1)
    vis = (key < pos) if strict else (key <= pos)
    return vis & (key < n_new)


def _negc_kernel(pt_ref, lfnew_ref, tri_ref, *refs, n_pages):
    lf_refs = refs[:n_pages]
    o_ref, onew_ref = refs[n_pages:]
    tri = tri_ref[...]
    both = _dot_exact01(jnp.concatenate([r[...] for r in lf_refs], axis=0), tri)
    car = jnp.zeros((FOX_HEADS, PAGE_SIZE), F32)
    for j in range(n_pages):
        rows = slice(FOX_HEADS * j, FOX_HEADS * (j + 1))
        o_ref[:, PAGE_SIZE * j: PAGE_SIZE * (j + 1)] = -(both[rows, :PAGE_SIZE] + car)
        car = car + both[rows, PAGE_SIZE:]
    onew_ref[...] = -(_dot_exact01(lfnew_ref[...], tri)[:, :PAGE_SIZE] + car)


def _negc(pt, lfnew_t, tri, cache_lf_t, layer):
    nb = lfnew_t.shape[0]
    pages_per_seq = pt.shape[0] // nb

    def page_spec(j):
        return pl.BlockSpec((None, None, FOX_HEADS, PAGE_SIZE),
                            lambda b, pt: (layer, pt[b * pages_per_seq + j], 0, 0))

    grid_spec = pltpu.PrefetchScalarGridSpec(
        num_scalar_prefetch=1,
        grid=(nb,),
        in_specs=[pl.BlockSpec((None, FOX_HEADS, PAGE_SIZE), lambda b, pt: (b, 0, 0)),
                  pl.BlockSpec((PAGE_SIZE, 2 * PAGE_SIZE), lambda b, pt: (0, 0))]
                 + [page_spec(j) for j in range(pages_per_seq)],
        out_specs=(pl.BlockSpec((None, FOX_HEADS, pages_per_seq * PAGE_SIZE), lambda b, pt: (b, 0, 0)),
                   pl.BlockSpec((None, FOX_HEADS, PAGE_SIZE), lambda b, pt: (b, 0, 0))),
    )
    return pl.pallas_call(
        functools.partial(_negc_kernel, n_pages=pages_per_seq),
        grid_spec=grid_spec,
        out_shape=(jax.ShapeDtypeStruct((nb, FOX_HEADS, pages_per_seq * PAGE_SIZE), F32),
                   jax.ShapeDtypeStruct((nb, FOX_HEADS, PAGE_SIZE), F32)),
        compiler_params=_cparams(("arbitrary",), 16 << 20),
        name="negc",
    )(pt, lfnew_t, tri, *([cache_lf_t] * pages_per_seq))


def _page_specs(n_pages, pages_per_seq, shape, layer, reverse=False, rightmost_only=False):
    specs = []
    for j in range(n_pages):
        if rightmost_only:
            def imap(b, c, pt, _j=j):
                return (layer, pt[b * pages_per_seq + pages_per_seq - n_pages + _j], 0, 0)
        elif reverse:
            def imap(b, c, pt, _j=j):
                return (layer, pt[b * pages_per_seq + pages_per_seq - (c + 1) * n_pages + _j], 0, 0)
        else:
            def imap(b, c, pt, _j=j):
                return (layer, pt[b * pages_per_seq + c * n_pages + _j], 0, 0)
        specs.append(pl.BlockSpec((None, None) + tuple(shape), imap))
    return specs


def _decode_kernel(pt_ref, fq_ref, sq_ref, ql_ref, qr_ref, fkn_ref, fvn_ref, negcn_ref, skn_ref, svn_ref,
                   cn_ref, rn_ref, negc_ref, urev_ref, *refs, n_pages, n_new, sb_rightmost_only):
    fk, fv, sk, sv, cc, rr = (refs[i * n_pages:(i + 1) * n_pages] for i in range(6))
    (fo_ref, so_ref, lo_ref, ro_ref,
     fm_ref, fl_ref, fa_ref, sr_ref, sa_ref, mm_ref, ml_ref, ma_ref) = refs[6 * n_pages:]
    c = pl.program_id(1)
    last = pl.num_programs(1) - 1
    rows = fq_ref.shape[0]
    reps = rows // FOX_HEADS
    fq = fq_ref[...]
    sq = sq_ref[...]
    ql = ql_ref[...]
    qr = qr_ref[...]
    urev = urev_ref[...]

    def cat_lanes(page_refs):
        return jnp.concatenate([r[...].astype(BF16) for r in page_refs], axis=1)

    def tile_rows(x):
        return jnp.concatenate([x] * reps, axis=0)

    def log_rest(z):
        return -(jnp.maximum(z, 0.0) + jnp.log1p(jnp.exp(-jnp.abs(z))))

    def rev_cumsum(lr):
        lr1 = lr.astype(BF16)
        lr2 = (lr - lr1.astype(F32)).astype(BF16)
        return (jnp.dot(lr1, urev, preferred_element_type=F32)
                + jnp.dot(lr2, urev, preferred_element_type=F32))

    @pl.when(c == 0)
    def _():
        for m_ref, l_ref, a_ref in ((fm_ref, fl_ref, fa_ref), (mm_ref, ml_ref, ma_ref)):
            m_ref[...] = jnp.full_like(m_ref, NEG_INF)
            l_ref[...] = jnp.zeros_like(l_ref)
            a_ref[...] = jnp.zeros_like(a_ref)
        keep = _new_row_mask(rows, n_new, True)
        z = jnp.dot(sq, skn_ref[...].astype(BF16), preferred_element_type=F32)
        both = rev_cumsum(jnp.where(keep, log_rest(z), 0.0))
        w = jnp.where(keep, jnp.exp(z + both[:, :PAGE_SIZE]), 0.0)
        sr_ref[...] = both[:, PAGE_SIZE:]
        sa_ref[...] = _dot_nt(w.astype(BF16), svn_ref[...].astype(BF16))

    s = jnp.dot(fq, cat_lanes(fk), preferred_element_type=F32) + tile_rows(negc_ref[...])
    _softmax_update(s, cat_lanes(fv), fm_ref, fl_ref, fa_ref, v_key_minor=True)

    cb = jnp.concatenate([r[...].astype(BF16) for r in cc], axis=0)
    s = _dot_nt(ql, cb) + jnp.dot(qr, cat_lanes(rr), preferred_element_type=F32)
    _softmax_update(s, cb, mm_ref, ml_ref, ma_ref, v_key_minor=False)

    @pl.when((c == 0) if sb_rightmost_only else (jnp.max(sr_ref[...]) > SB_DONE))
    def _():
        z = jnp.dot(sq, cat_lanes(sk), preferred_element_type=F32)
        lr = log_rest(z)
        both = rev_cumsum(jnp.concatenate(
            [lr[:, PAGE_SIZE * j: PAGE_SIZE * (j + 1)] for j in range(n_pages)], axis=0))
        r = sr_ref[...]
        xs = [None] * n_pages
        for j in reversed(range(n_pages)):
            blk = both[rows * j: rows * (j + 1), :]
            xs[j] = z[:, PAGE_SIZE * j: PAGE_SIZE * (j + 1)] + blk[:, :PAGE_SIZE] + r
            r = r + blk[:, PAGE_SIZE:]
        sr_ref[...] = r
        w = jnp.exp(jnp.concatenate(xs, axis=1)).astype(BF16)
        sa_ref[...] += _dot_nt(w, cat_lanes(sv))

    @pl.when(c == last)
    def _():
        s_new = jnp.dot(fq, fkn_ref[...].astype(BF16), preferred_element_type=F32) + tile_rows(negcn_ref[...])
        s_new = jnp.where(_new_row_mask(rows, n_new, False), s_new, NEG_INF)
        _softmax_update(s_new, fvn_ref[...].astype(BF16), fm_ref, fl_ref, fa_ref, v_key_minor=True)
        fo_ref[...] = fa_ref[...] / fl_ref[...]

        cn = cn_ref[...].astype(BF16)
        s_new = _dot_nt(ql, cn) + jnp.dot(qr, rn_ref[...].astype(BF16), preferred_element_type=F32)
        s_new = jnp.where(_new_row_mask(rows, n_new, False), s_new, NEG_INF)
        _softmax_update(s_new, cn, mm_ref, ml_ref, ma_ref, v_key_minor=False)
        lo_ref[...] = ma_ref[...] / ml_ref[:, :1]

        so_ref[...] = sa_ref[...]
        ro_ref[...] = sr_ref[...]


def _decode(pt, fq, sq, ql, qr, fkn, fvn, negcn, skn, svn, cn, rn, negc, urev,
            fk_t, fv_t, sk_t, sv_t, ckv, kr_t, layer, n_new, sb_rightmost_only):
    nb, rows, _ = fq.shape
    pages_per_seq = pt.shape[0] // nb
    n_pages = PAGES_PER_STEP
    per_b = lambda r, w: pl.BlockSpec((None, r, w), lambda b, c, pt: (b, 0, 0))
    kv_t = (LANES, PAGE_SIZE)
    grid_spec = pltpu.PrefetchScalarGridSpec(
        num_scalar_prefetch=1,
        grid=(nb, pages_per_seq // n_pages),
        in_specs=[per_b(rows, LANES), per_b(rows, LANES), per_b(rows, KV_LORA), per_b(rows, ROPE_DIM),
                  per_b(*kv_t), per_b(*kv_t), per_b(FOX_HEADS, PAGE_SIZE), per_b(*kv_t), per_b(*kv_t),
                  per_b(PAGE_SIZE, KV_LORA), per_b(ROPE_DIM, PAGE_SIZE),
                  pl.BlockSpec((None, FOX_HEADS, n_pages * PAGE_SIZE), lambda b, c, pt: (b, 0, c)),
                  pl.BlockSpec((PAGE_SIZE, 2 * PAGE_SIZE), lambda b, c, pt: (0, 0))]
                 + _page_specs(n_pages, pages_per_seq, kv_t, layer)
                 + _page_specs(n_pages, pages_per_seq, kv_t, layer)
                 + _page_specs(n_pages, pages_per_seq, kv_t, layer, reverse=True, rightmost_only=sb_rightmost_only)
                 + _page_specs(n_pages, pages_per_seq, kv_t, layer, reverse=True, rightmost_only=sb_rightmost_only)
                 + _page_specs(n_pages, pages_per_seq, (PAGE_SIZE, KV_LORA), layer)
                 + _page_specs(n_pages, pages_per_seq, (ROPE_DIM, PAGE_SIZE), layer),
        out_specs=(per_b(rows, LANES), per_b(rows, LANES), per_b(rows, KV_LORA), per_b(rows, LANES)),
        scratch_shapes=[pltpu.VMEM((rows, LANES), F32), pltpu.VMEM((rows, LANES), F32),
                        pltpu.VMEM((rows, LANES), F32),
                        pltpu.VMEM((rows, LANES), F32), pltpu.VMEM((rows, LANES), F32),
                        pltpu.VMEM((rows, LANES), F32), pltpu.VMEM((rows, LANES), F32),
                        pltpu.VMEM((rows, KV_LORA), F32)],
    )
    return pl.pallas_call(
        functools.partial(_decode_kernel, n_pages=n_pages, n_new=n_new, sb_rightmost_only=sb_rightmost_only),
        grid_spec=grid_spec,
        out_shape=(jax.ShapeDtypeStruct((nb, rows, LANES), F32),
                   jax.ShapeDtypeStruct((nb, rows, LANES), F32),
                   jax.ShapeDtypeStruct((nb, rows, KV_LORA), F32),
                   jax.ShapeDtypeStruct((nb, rows, LANES), F32)),
        compiler_params=_cparams(("arbitrary", "arbitrary"), 40 << 20),
        name="decode_sb_rightmost" if sb_rightmost_only else "decode",
    )(pt, fq, sq, ql, qr, fkn, fvn, negcn, skn, svn, cn, rn, negc, urev,
      *([fk_t] * n_pages), *([fv_t] * n_pages), *([sk_t] * n_pages), *([sv_t] * n_pages),
      *([ckv] * n_pages), *([kr_t] * n_pages))


def _pack_w_in(w_in_l):
    d = w_in_l.shape[0]
    src = {}
    off = 0
    for name, w in (("fq", 512), ("fk", 128), ("fv", 128), ("ff", 8), ("sq", 512), ("sk", 128),
                    ("sv", 128), ("cq", 512), ("ckv", 512), ("kr", 64)):
        src[name] = w_in_l[:, off:off + w]
        off += w
    cols = []
    for name in ("fq", "sq", "cq", "ckv", "fk", "fv", "sk", "sv", "kr", "ff"):
        seg = src[name]
        pad = _SEG[name][1] - seg.shape[1]
        if pad:
            seg = jnp.concatenate([seg, jnp.zeros((d, pad), seg.dtype)], axis=1)
        cols.append(seg)
    return jnp.concatenate(cols, axis=1).astype(BF16)


def _rope_tables(pos):
    half = ROPE_DIM // 2
    inv_freq = ROPE_THETA ** (-jnp.arange(half, dtype=F32) / half)
    ang = pos.astype(F32)[:, None] * inv_freq[None, :]
    cos = jnp.cos(ang)
    sin = jnp.sin(ang)
    return jnp.tile(cos, (1, 4)), jnp.tile(jnp.concatenate([-sin, sin], axis=1), (1, 2))


def _block_diag_queries(q, nb, n_new):
    group = FOX_HEADS // FOX_KV_HEADS
    q = (q * ATTN_SCALE).reshape(nb, n_new, FOX_KV_HEADS, group, HEAD_DIM)
    eye = jnp.eye(FOX_KV_HEADS, dtype=q.dtype)
    qbd = q[:, :, :, :, None, :] * eye[None, None, :, None, :, None]
    return qbd.reshape(nb, n_new * FOX_HEADS, FOX_KV_HEADS * HEAD_DIM).astype(BF16)


def _pad_new_rows(x, nb, n_new):
    w = x.shape[1]
    x = x.reshape(nb, n_new, w)
    return jnp.concatenate([x, jnp.zeros((nb, PAGE_SIZE - n_new, w), x.dtype)], axis=1)


def _pad_new_rows_t(x, nb, n_new):
    return jnp.swapaxes(_pad_new_rows(x, nb, n_new), 1, 2)


def _pages_key_minor(cache):
    depth, pool, page, kvh, hd = cache.shape
    return jnp.transpose(cache, (0, 1, 3, 4, 2)).reshape(depth, pool, kvh * hd, page)


def _heads_from_rows(o, nb, n_new):
    group = FOX_HEADS // FOX_KV_HEADS
    o = o.reshape(nb, n_new, FOX_KV_HEADS, group, FOX_KV_HEADS, HEAD_DIM)
    o = jnp.stack([o[:, :, g, :, g, :] for g in range(FOX_KV_HEADS)], axis=2)
    return o.reshape(nb * n_new, FOX_HEADS * HEAD_DIM)


def kernel(x_prompt, x_sample, cache_fox_k, cache_fox_v, cache_fox_logf, cache_sb_k, cache_sb_v,
           cache_mla_ckv, cache_mla_krope, page_table, w_in, b_f, g_cq, w_uq, g_ckv, w_uk, w_uv,
           g_o, w_o, ln1_g, ln1_b, w_ff1, w_ff2, ln2_g, ln2_b):
    batch, seq, d_model = x_prompt.shape
    nb, n_new, _ = x_sample.shape
    depth = w_in.shape[0]
    n_pool = cache_fox_k.shape[1]
    pages_per_seq = page_table.shape[1]
    past_len = pages_per_seq * PAGE_SIZE
    rows_p = batch * seq
    rows_s = nb * n_new
    assert rows_s == ROW_TILE and seq % ROW_TILE == 0 and pages_per_seq % PAGES_PER_STEP == 0

    x = jnp.concatenate([x_prompt.reshape(rows_p, d_model), x_sample.reshape(rows_s, d_model)], axis=0)
    pos = jnp.concatenate([jnp.tile(jnp.arange(seq), batch), jnp.tile(past_len + jnp.arange(n_new), nb)])
    cos_t, sin_t = _rope_tables(pos)
    pt = page_table.reshape(-1)

    fk_t = _pages_key_minor(cache_fox_k)
    fv_t = _pages_key_minor(cache_fox_v)
    sk_t = _pages_key_minor(cache_sb_k)
    sv_t = _pages_key_minor(cache_sb_v)
    lf_t = jnp.swapaxes(cache_fox_logf, 2, 3)
    kr_t = jnp.swapaxes(cache_mla_krope, 2, 3)

    r_i = lax.broadcasted_iota(jnp.int32, (ROW_TILE, ROW_TILE), 0)
    c_i = lax.broadcasted_iota(jnp.int32, (ROW_TILE, ROW_TILE), 1)
    tri512 = (r_i <= c_i).astype(BF16)
    urev256 = (r_i >= c_i).astype(BF16)[:256, :256]
    ones128 = jnp.ones((PAGE_SIZE, PAGE_SIZE), BF16)
    tri128 = jnp.concatenate([tri512[:PAGE_SIZE, :PAGE_SIZE], ones128], axis=1)
    urev128 = jnp.concatenate([urev256[:PAGE_SIZE, :PAGE_SIZE], ones128], axis=1)

    rows_out_p = [[] for _ in range(7)]
    rows_out_s = [[] for _ in range(7)]
    for l in range(depth):
        w_in_p = _pack_w_in(w_in[l])
        wuq3 = w_uq[l].reshape(Q_LORA, MLA_HEADS, NOPE_DIM + ROPE_DIM)
        wuq_p = jnp.concatenate([wuq3[:, :, :NOPE_DIM].reshape(Q_LORA, -1),
                                 wuq3[:, :, NOPE_DIM:].reshape(Q_LORA, -1)], axis=1).astype(BF16)
        wuk_f = w_uk[l].reshape(KV_LORA, MLA_HEADS * NOPE_DIM).astype(BF16)
        wuv_f = w_uv[l].reshape(KV_LORA, MLA_HEADS * V_DIM).astype(BF16)
        wuk_t = jnp.transpose(w_uk[l], (1, 2, 0)).astype(BF16)
        wuv_h = jnp.transpose(w_uv[l], (1, 0, 2)).astype(BF16)
        bf_row = jnp.zeros((1, LANES), F32).at[0, :FOX_HEADS].set(b_f[l])

        u = _proj(x, w_in_p)
        logf, cs, ckvn, krope, qm, km, vm = _prep(
            u, bf_row, g_cq[l][None, :], g_ckv[l][None, :], wuq_p, wuk_f, wuv_f, cos_t, sin_t, tri512,
            seq // ROW_TILE)

        fox_p = _fox_prefill(u, cs, batch, seq)
        sb_p = _sb_prefill(u, urev256, batch, seq)
        mla_p = _mla_prefill(qm, km, vm, batch, seq)

        def seg_s(name, width=None):
            off, w = _SEG[name]
            return u[rows_p:, off:off + (width or w)]

        fox_q = _block_diag_queries(seg_s("fq"), nb, n_new)
        sb_q = _block_diag_queries(seg_s("sq"), nb, n_new)
        negc, negc_new = _negc(pt, _pad_new_rows_t(logf[rows_p:], nb, n_new), tri128, lf_t, l)
        qlat = _qlat(qm, wuk_t, rows_p // rows_s, rows_s)
        qlat = jnp.transpose(qlat.reshape(MLA_HEADS, nb, n_new, KV_LORA), (1, 2, 0, 3))
        qlat = qlat.reshape(nb, n_new * MLA_HEADS, KV_LORA)
        qrope = qm[rows_p:].reshape(nb, n_new * MLA_HEADS, 256)[:, :, NOPE_DIM:NOPE_DIM + ROPE_DIM]
        decode_args = (
            pt, fox_q, sb_q, qlat, qrope,
            _pad_new_rows_t(seg_s("fk"), nb, n_new), _pad_new_rows_t(seg_s("fv"), nb, n_new), negc_new,
            _pad_new_rows_t(seg_s("sk"), nb, n_new), _pad_new_rows_t(seg_s("sv"), nb, n_new),
            _pad_new_rows(ckvn[rows_p:], nb, n_new), _pad_new_rows_t(krope[rows_p:], nb, n_new),
            negc, urev128, fk_t, fv_t, sk_t, sv_t, cache_mla_ckv, kr_t)
        fox_s, sb_s, lat, sb_r = _decode(*decode_args, l, n_new, sb_rightmost_only=True)
        fox_s, sb_s, lat = lax.cond(
            jnp.max(sb_r) > SB_DONE,
            lambda args: _decode(*args[0], l, n_new, sb_rightmost_only=False)[:3],
            lambda args: args[1],
            (decode_args, (fox_s, sb_s, lat)))
        lat = jnp.transpose(lat.reshape(nb, n_new, MLA_HEADS, KV_LORA), (2, 0, 1, 3))
        mla_s = _latv(lat.reshape(MLA_HEADS, rows_s, KV_LORA), wuv_h)

        fox_o = jnp.concatenate([fox_p, _heads_from_rows(fox_s, nb, n_new)], axis=0)
        sb_o = jnp.concatenate([sb_p, _heads_from_rows(sb_s, nb, n_new)], axis=0)
        mla_o = jnp.concatenate([mla_p, mla_s], axis=0)

        x1, x1b = _outproj(fox_o, sb_o, mla_o, x, g_o[l][None, :], w_o[l].astype(BF16),
                           ln1_g[l][None, :], ln1_b[l][None, :])
        h = _ff1(x1b, w_ff1[l].astype(BF16))
        x = _ff2(h, w_ff2[l].astype(BF16), x1, ln2_g[l][None, :], ln2_b[l][None, :])

        def seg_u(name, width=None):
            off, w = _SEG[name]
            return u[:, off:off + (width or w)]

        leaves = (seg_u("fk").reshape(-1, FOX_KV_HEADS, HEAD_DIM),
                  seg_u("fv").reshape(-1, FOX_KV_HEADS, HEAD_DIM),
                  logf,
                  seg_u("sk").reshape(-1, SB_KV_HEADS, HEAD_DIM),
                  seg_u("sv").reshape(-1, SB_KV_HEADS, HEAD_DIM),
                  ckvn, krope)
        for j, leaf in enumerate(leaves):
            rows_out_p[j].append(leaf[:rows_p].reshape((batch, seq) + leaf.shape[1:]))
            rows_out_s[j].append(leaf[rows_p:].reshape((nb, n_new) + leaf.shape[1:]))

    y_p = x[:rows_p].reshape(batch, seq, d_model)
    y_s = x[rows_p:].reshape(nb, n_new, d_model)
    return (y_p, y_s) + tuple(jnp.stack(r) for r in rows_out_p) + tuple(jnp.stack(r) for r in rows_out_s)
```
